```python
import jax, jax.numpy as jnp
from jax import lax
import numpy as np

D_MODEL = 2048
BATCH = 4
SEQ = 2048
DEPTH = 1
DEC_BATCH = 128
DEC_SEQ = 4
PAST_LEN = 8192
PAGE_SIZE = 128

EPS = 1e-6
D_RNN = D_MODEL // 2
LRU_BLOCKS = 16
LRU_BLOCK_W = D_RNN // LRU_BLOCKS
CONV_W = 4
LRU_C = 8.0
MLA_HEADS = 16
Q_LORA = D_MODEL // 4
KV_LORA = D_MODEL // 8
NOPE_DIM = 128
ROPE_DIM = 64
V_DIM = 128
ROPE_THETA = 10000.0
Q_BLOCK = 128
MEM_TOKENS = 256
MEM_HEADS = 4
MEM_HEAD_DIM = D_MODEL // 8
N_BRANCH = 3
PEER_HEADS = 8
PEER_KEYS = 128
PEER_EXPERTS = PEER_KEYS * PEER_KEYS
PEER_KEY_DIM = 128
PEER_TOPK = 16
PEER_BLOCK = 128

kernel_name = "hybrid_rglru_mla_mem_peer_step"


def rmsnorm(x, g):
    xf = x.astype(jnp.float32)
    y = xf * lax.rsqrt(jnp.mean(xf * xf, axis=-1, keepdims=True) + EPS)
    return (y * g.astype(jnp.float32)).astype(x.dtype)


def rope(x, pos):
    half = ROPE_DIM // 2
    inv = ROPE_THETA ** (-jnp.arange(half, dtype=jnp.float32) / half)
    ang = pos.astype(jnp.float32)[:, None] * inv[None, :]
    ang = ang.reshape((ang.shape[0],) + (1,) * (x.ndim - 3) + (half,))
    cos, sin = jnp.cos(ang).astype(x.dtype), jnp.sin(ang).astype(x.dtype)
    x1, x2 = x[..., :half], x[..., half:]
    return jnp.concatenate([x1 * cos - x2 * sin, x1 * sin + x2 * cos], axis=-1)


def causal_conv(xr, conv0, conv_w, conv_b):
    s = xr.shape[1]
    xcat = jnp.concatenate([conv0.astype(xr.dtype), xr], axis=1)
    out = conv_b + sum(xcat[:, k:k + s] * conv_w[k] for k in range(CONV_W))
    return out, xcat[:, s:]


def rglru(xc, h0, w_rec_a, b_rec_a, w_rec_x, b_rec_x, lru_lambda):
    b, s, _ = xc.shape
    xf = xc.astype(jnp.float32)
    xb = xf.reshape(b, s, LRU_BLOCKS, LRU_BLOCK_W)
    r = jax.nn.sigmoid(jnp.einsum('bsni,nij->bsnj', xb, w_rec_a.astype(jnp.float32)).reshape(b, s, D_RNN) + b_rec_a.astype(jnp.float32))
    i = jax.nn.sigmoid(jnp.einsum('bsni,nij->bsnj', xb, w_rec_x.astype(jnp.float32)).reshape(b, s, D_RNN) + b_rec_x.astype(jnp.float32))
    log_a = -LRU_C * r * jax.nn.softplus(-lru_lambda.astype(jnp.float32))
    a = jnp.exp(log_a)
    u = jnp.sqrt(-jnp.expm1(2.0 * log_a)) * (i * xf)

    def step(h, au):
        a_t, u_t = au
        h = a_t * h + u_t
        return h, h

    h_last, hs = lax.scan(step, h0.astype(jnp.float32), (jnp.swapaxes(a, 0, 1), jnp.swapaxes(u, 0, 1)))
    return jnp.swapaxes(hs, 0, 1).astype(xc.dtype), h_last.astype(xc.dtype)


def mla_attend(q_abs, q_pe, ckv, kpe, q_pos, k_pos):
    scale = (NOPE_DIM + ROPE_DIM) ** -0.5

    def one_block(blk):
        qa, qp, pos = blk
        s = (jnp.einsum('bqhc,bkc->bhqk', qa, ckv).astype(jnp.float32)
             + jnp.einsum('bqhr,bkr->bhqk', qp, kpe).astype(jnp.float32)) * scale
        s = jnp.where(k_pos[None, :] <= pos[:, None], s, -jnp.inf)
        p = jax.nn.softmax(s, axis=-1).astype(ckv.dtype)
        return jnp.einsum('bhqk,bkc->bqhc', p, ckv)

    b, sq = q_abs.shape[:2]
    if sq > Q_BLOCK and sq % Q_BLOCK == 0:
        nb = sq // Q_BLOCK

        def split(t):
            return jnp.moveaxis(t.reshape((b, nb, Q_BLOCK) + t.shape[2:]), 1, 0)

        o = lax.map(one_block, (split(q_abs), split(q_pe), q_pos.reshape(nb, Q_BLOCK)))
        return jnp.moveaxis(o, 0, 1).reshape(q_abs.shape)
    return one_block((q_abs, q_pe, q_pos))


def mem_attend(q, k, v):
    s = jnp.einsum('bshd,bmhd->bhsm', q, k).astype(jnp.float32) * (MEM_HEAD_DIM ** -0.5)
    p = jax.nn.softmax(s, axis=-1).astype(v.dtype)
    return jnp.einsum('bhsm,bmhd->bshd', p, v)


def peer(h, w_peer_query, peer_sub_keys, peer_u, peer_v):
    shp = h.shape
    h2 = h.reshape(-1, D_MODEL)
    t = h2.shape[0]
    nb = -(-t // PEER_BLOCK)
    hp = jnp.pad(h2, ((0, nb * PEER_BLOCK - t), (0, 0))).reshape(nb, PEER_BLOCK, D_MODEL)

    def one_block(hb):
        tb = hb.shape[0]
        q = jnp.einsum('td,dhpk->thpk', hb, w_peer_query)
        s = jnp.einsum('thpk,hpnk->thpn', q, peer_sub_keys).astype(jnp.float32)
        s1, i1 = lax.top_k(s[:, :, 0], PEER_TOPK)
        s2, i2 = lax.top_k(s[:, :, 1], PEER_TOPK)
        cand = (s1[..., :, None] + s2[..., None, :]).reshape(tb, PEER_HEADS, PEER_TOPK * PEER_TOPK)
        cidx = (i1[..., :, None] * PEER_KEYS + i2[..., None, :]).reshape(tb, PEER_HEADS, PEER_TOPK * PEER_TOPK)
        top, sel = lax.top_k(cand, PEER_TOPK)
        idx = jnp.take_along_axis(cidx, sel, axis=-1)
        w = jax.nn.softmax(top, axis=-1)
        act = jax.nn.gelu(jnp.einsum('thkd,td->thk', peer_u[idx], hb).astype(jnp.float32))
        return jnp.einsum('thk,thkd->td', (w * act).astype(hb.dtype), peer_v[idx])

    out = lax.map(one_block, hp)
    return out.reshape(-1, D_MODEL)[:t].reshape(shp)


def hybrid_layer(x, mem_k, mem_v, h0, conv0, ckv_past, krope_past,
                 g_mix, w_in, conv_w, conv_b, w_rec_a, b_rec_a, w_rec_x, b_rec_x, lru_lambda, w_o_lru,
                 g_q_a, w_q_b, g_kv_a, w_kv_b, w_o_mla, w_o_mem, w_out,
                 g_ffn, w_peer_query, peer_sub_keys, peer_u, peer_v, g_final):
    b, s, _ = x.shape
    p_len = ckv_past.shape[1]
    h = rmsnorm(x, g_mix)
    proj = h @ w_in
    offsets = np.cumsum([D_RNN, D_RNN, Q_LORA, KV_LORA + ROPE_DIM, MEM_HEADS * MEM_HEAD_DIM]).tolist()
    xr, gr, qa, kva, qm, gates = jnp.split(proj, offsets, axis=-1)

    xc, conv_last = causal_conv(xr, conv0, conv_w, conv_b)
    y_lru, h_last = rglru(xc, h0, w_rec_a, b_rec_a, w_rec_x, b_rec_x, lru_lambda)
    y_lru = y_lru * jax.nn.gelu(gr)

    q_pos = p_len + jnp.arange(s)
    q = jnp.einsum('bsc,chd->bshd', rmsnorm(qa, g_q_a), w_q_b)
    q_nope, q_pe = q[..., :NOPE_DIM], rope(q[..., NOPE_DIM:], q_pos)
    ckv_new = rmsnorm(kva[..., :KV_LORA], g_kv_a)
    krope_new = rope(kva[..., KV_LORA:], q_pos)
    ckv_all = jnp.concatenate([ckv_past.astype(x.dtype), ckv_new], axis=1)
    krope_all = jnp.concatenate([krope_past.astype(x.dtype), krope_new], axis=1)
    k_pos = jnp.arange(p_len + s)
    q_abs = jnp.einsum('bshn,chn->bshc', q_nope, w_kv_b[..., :NOPE_DIM])
    o_lat = mla_attend(q_abs, q_pe, ckv_all, krope_all, q_pos, k_pos)
    o_mla = jnp.einsum('bshc,chv->bshv', o_lat, w_kv_b[..., NOPE_DIM:]).reshape(b, s, MLA_HEADS * V_DIM)

    o_mem = mem_attend(qm.reshape(b, s, MEM_HEADS, MEM_HEAD_DIM), mem_k, mem_v).reshape(b, s, MEM_HEADS * MEM_HEAD_DIM)

    g_l, g_a, g_m = jnp.split(jax.nn.sigmoid(gates), N_BRANCH, axis=-1)
    merged = g_l * (y_lru @ w_o_lru) + g_a * (o_mla @ w_o_mla) + g_m * (o_mem @ w_o_mem)
    x = x + merged @ w_out
    x = x + peer(rmsnorm(x, g_ffn), w_peer_query, peer_sub_keys, peer_u, peer_v)
    return rmsnorm(x, g_final), ckv_new, krope_new, h_last, conv_last


def setup_inputs(seed: int = 0) -> dict:
    key = jax.random.key(seed)
    ks = jax.random.split(key, 40)
    f32 = jnp.float32

    def nrm(k, shape, scale):
        return jax.random.normal(k, shape, f32) * scale

    def gain(k, d):
        return 1.0 + 0.01 * jax.random.normal(k, (d,), f32)

    n_pages = PAST_LEN // PAGE_SIZE
    n_used = DEC_BATCH * n_pages
    n_pool = n_used + (n_used + 3) // 4
    page_table = jax.random.permutation(ks[0], n_pool)[:n_used].reshape(DEC_BATCH, n_pages).astype(jnp.int32)
    in_width = 2 * D_RNN + Q_LORA + KV_LORA + ROPE_DIM + MEM_HEADS * MEM_HEAD_DIM + N_BRANCH * D_MODEL
    a_c = jax.random.uniform(ks[1], (D_RNN,), f32, minval=0.9, maxval=0.999)
    p_a = a_c ** (1.0 / LRU_C)
    lru_lambda = jnp.log(p_a) - jnp.log1p(-p_a)
    return {
        'x_prompt': nrm(ks[2], (BATCH, SEQ, D_MODEL), 1.0),
        'x_sample': nrm(ks[3], (DEC_BATCH, DEC_SEQ, D_MODEL), 1.0),
        'cache_mla_ckv': nrm(ks[4], (n_pool, PAGE_SIZE, KV_LORA), 1.0),
        'cache_mla_krope': nrm(ks[5], (n_pool, PAGE_SIZE, ROPE_DIM), 1.0),
        'cache_mem_k': nrm(ks[6], (DEC_BATCH, MEM_TOKENS, MEM_HEADS, MEM_HEAD_DIM), 1.0),
        'cache_mem_v': nrm(ks[7], (DEC_BATCH, MEM_TOKENS, MEM_HEADS, MEM_HEAD_DIM), 1.0),
        'state_lru_h': nrm(ks[8], (DEC_BATCH, D_RNN), 0.5),
        'state_lru_conv': nrm(ks[9], (DEC_BATCH, CONV_W - 1, D_RNN), 1.0),
        'page_table': page_table,
        'mem_prompt': nrm(ks[10], (BATCH, MEM_TOKENS, D_MODEL), 1.0),
        'g_mix': gain(ks[11], D_MODEL),
        'w_in': nrm(ks[12], (D_MODEL, in_width), D_MODEL ** -0.5),
        'conv_w': nrm(ks[13], (CONV_W, D_RNN), CONV_W ** -0.5),
        'conv_b': nrm(ks[14], (D_RNN,), 0.01),
        'w_rec_a': nrm(ks[15], (LRU_BLOCKS, LRU_BLOCK_W, LRU_BLOCK_W), LRU_BLOCK_W ** -0.5),
        'b_rec_a': nrm(ks[16], (D_RNN,), 0.01),
        'w_rec_x': nrm(ks[17], (LRU_BLOCKS, LRU_BLOCK_W, LRU_BLOCK_W), LRU_BLOCK_W ** -0.5),
        'b_rec_x': nrm(ks[18], (D_RNN,), 0.01),
        'lru_lambda': lru_lambda,
        'w_o_lru': nrm(ks[19], (D_RNN, D_MODEL), D_RNN ** -0.5),
        'g_q_a': gain(ks[20], Q_LORA),
        'w_q_b': nrm(ks[21], (Q_LORA, MLA_HEADS, NOPE_DIM + ROPE_DIM), Q_LORA ** -0.5),
        'g_kv_a': gain(ks[22], KV_LORA),
        'w_kv_b': nrm(ks[23], (KV_LORA, MLA_HEADS, NOPE_DIM + V_DIM), KV_LORA ** -0.5),
        'w_o_mla': nrm(ks[24], (MLA_HEADS * V_DIM, D_MODEL), (MLA_HEADS * V_DIM) ** -0.5),
        'g_mem': gain(ks[25], D_MODEL),
        'w_mem_k': nrm(ks[26], (D_MODEL, MEM_HEADS, MEM_HEAD_DIM), D_MODEL ** -0.5),
        'w_mem_v': nrm(ks[27], (D_MODEL, MEM_HEADS, MEM_HEAD_DIM), D_MODEL ** -0.5),
        'w_o_mem': nrm(ks[28], (MEM_HEADS * MEM_HEAD_DIM, D_MODEL), (MEM_HEADS * MEM_HEAD_DIM) ** -0.5),
        'w_out': nrm(ks[29], (D_MODEL, D_MODEL), D_MODEL ** -0.5),
        'g_ffn': gain(ks[30], D_MODEL),
        'w_peer_query': nrm(ks[31], (D_MODEL, PEER_HEADS, 2, PEER_KEY_DIM), D_MODEL ** -0.5),
        'peer_sub_keys': nrm(ks[32], (PEER_HEADS, 2, PEER_KEYS, PEER_KEY_DIM), PEER_KEY_DIM ** -0.5),
        'peer_u': nrm(ks[33], (PEER_EXPERTS, D_MODEL), D_MODEL ** -0.5),
        'peer_v': nrm(ks[34], (PEER_EXPERTS, D_MODEL), PEER_HEADS ** -0.5),
        'g_final': gain(ks[35], D_MODEL),
    }


def reference(x_prompt, x_sample, cache_mla_ckv, cache_mla_krope, cache_mem_k, cache_mem_v,
              state_lru_h, state_lru_conv, page_table, mem_prompt,
              g_mix, w_in, conv_w, conv_b, w_rec_a, b_rec_a, w_rec_x, b_rec_x, lru_lambda, w_o_lru,
              g_q_a, w_q_b, g_kv_a, w_kv_b, w_o_mla, g_mem, w_mem_k, w_mem_v, w_o_mem, w_out,
              g_ffn, w_peer_query, peer_sub_keys, peer_u, peer_v, g_final):
    weights = (g_mix, w_in, conv_w, conv_b, w_rec_a, b_rec_a, w_rec_x, b_rec_x, lru_lambda, w_o_lru,
               g_q_a, w_q_b, g_kv_a, w_kv_b, w_o_mla, w_o_mem, w_out,
               g_ffn, w_peer_query, peer_sub_keys, peer_u, peer_v, g_final)
    dt = x_prompt.dtype
    b = x_prompt.shape[0]
    mn = rmsnorm(mem_prompt, g_mem)
    mem_k_prompt = jnp.einsum('bmd,dhk->bmhk', mn, w_mem_k)
    mem_v_prompt = jnp.einsum('bmd,dhk->bmhk', mn, w_mem_v)
    y_prompt, ckv_prompt, krope_prompt, lru_h_prompt, lru_conv_prompt = hybrid_layer(
        x_prompt, mem_k_prompt, mem_v_prompt,
        jnp.zeros((b, D_RNN), dt), jnp.zeros((b, CONV_W - 1, D_RNN), dt),
        jnp.zeros((b, 0, KV_LORA), dt), jnp.zeros((b, 0, ROPE_DIM), dt), *weights)
    db = x_sample.shape[0]
    ckv_past = cache_mla_ckv[page_table].reshape(db, -1, KV_LORA)
    krope_past = cache_mla_krope[page_table].reshape(db, -1, ROPE_DIM)
    y_sample, ckv_sample, krope_sample, lru_h_sample, lru_conv_sample = hybrid_layer(
        x_sample, cache_mem_k, cache_mem_v, state_lru_h, state_lru_conv, ckv_past, krope_past, *weights)
    return (y_prompt, y_sample, ckv_prompt, krope_prompt, mem_k_prompt, mem_v_prompt,
            lru_h_prompt, lru_conv_prompt, ckv_sample, krope_sample, lru_h_sample, lru_conv_sample)
```

```python
import functools
import math

import jax
import jax.numpy as jnp
import numpy as np
from jax import lax
from jax.experimental import pallas as pl
from jax.experimental.pallas import tpu as pltpu

F32 = jnp.float32
BF16 = jnp.bfloat16

D_MODEL = 2048
EPS = 1e-6
D_RNN = 1024
LRU_BLOCKS = 16
CONV_W = 4
LRU_C = 8.0
MLA_HEADS = 16
Q_LORA = 512
KV_LORA = 256
NOPE_DIM = 128
ROPE_DIM = 64
V_DIM = 128
ROPE_THETA = 10000.0
MEM_TOKENS = 256
MEM_HEADS = 4
MEM_HEAD_DIM = 256
PEER_HEADS = 8
PEER_KEYS = 128
PEER_EXPERTS = PEER_KEYS * PEER_KEYS
PEER_TOPK = 16
PAGE_SIZE = 128

LANES = 128
QK_PAD = KV_LORA + LANES
ATTN_SCALE = (NOPE_DIM + ROPE_DIM) ** -0.5
MEM_SCALE = MEM_HEAD_DIM ** -0.5
NEG_INF = float("-inf")

OFF_GATES = 0
OFF_XR = 3 * D_MODEL
OFF_GR = OFF_XR + D_RNN
OFF_QM = OFF_GR + D_RNN
OFF_QA = OFF_QM + MEM_HEADS * MEM_HEAD_DIM
OFF_CKV = OFF_QA + Q_LORA
OFF_KR = OFF_CKV + KV_LORA
P_WIDTH = OFF_KR + 2 * LANES

VMEM_LIMIT = 56 * 1024 * 1024


def _params(n_grid):
    return pltpu.CompilerParams(dimension_semantics=("arbitrary",) * n_grid, vmem_limit_bytes=VMEM_LIMIT)


def _rms(x, g):
    ms = jnp.mean(x * x, axis=-1, keepdims=True)
    return x * lax.rsqrt(ms + EPS) * g


def _gelu(x):
    cdf = 0.5 * (1.0 + jnp.tanh(math.sqrt(2.0 / math.pi) * (x + 0.044715 * (x * x * x))))
    return x * cdf


def _dot(a, b):
    return jnp.dot(a, b, preferred_element_type=F32)


def _dot_nt(a, b):
    return lax.dot_general(a, b, (((1,), (1,)), ((), ())), preferred_element_type=F32)


def _dot_tn(a, b):
    return lax.dot_general(a, b, (((0,), (0,)), ((), ())), preferred_element_type=F32)


def _norm_matmul_kernel(x_ref, g_ref, w_ref, o_ref, h_scr):
    @pl.when(pl.program_id(1) == 0)
    def _():
        h_scr[...] = _rms(x_ref[...], g_ref[...]).astype(BF16)

    o_ref[...] = _dot(h_scr[...], w_ref[...])


def norm_matmul(x, g, w, tm, tn, name):
    t, d = x.shape
    n = w.shape[1]
    return pl.pallas_call(
        _norm_matmul_kernel,
        grid=(t // tm, n // tn),
        in_specs=[pl.BlockSpec((tm, d), lambda i, j: (i, 0)),
                  pl.BlockSpec((1, d), lambda i, j: (0, 0)),
                  pl.BlockSpec((d, tn), lambda i, j: (0, j))],
        out_specs=pl.BlockSpec((tm, tn), lambda i, j: (i, j)),
        out_shape=jax.ShapeDtypeStruct((t, n), F32),
        scratch_shapes=[pltpu.VMEM((tm, d), BF16)],
        compiler_params=_params(2),
        name=name,
    )(x, g.reshape(1, d), w)


def _lru_gates(xc, wa_ref, ba, wx_ref, bx, lam):
    xb = xc.astype(BF16)
    nk = wa_ref.shape[0]
    wk = D_RNN // nk
    ra = [_dot(xb[:, wk * k:wk * (k + 1)], wa_ref[k]) for k in range(nk)]
    ix = [_dot(xb[:, wk * k:wk * (k + 1)], wx_ref[k]) for k in range(nk)]
    r = jax.nn.sigmoid(jnp.concatenate(ra, axis=1) + ba)
    i = jax.nn.sigmoid(jnp.concatenate(ix, axis=1) + bx)
    neg_lam = -lam
    softplus = jnp.maximum(neg_lam, 0.0) + jnp.log1p(jnp.exp(-jnp.abs(neg_lam)))
    log_a = -LRU_C * r * softplus
    a = jnp.exp(log_a)
    u = jnp.sqrt(-jnp.tanh(log_a) * (a * a + 1.0)) * (i * xc)
    return a, u


def _lru_prompt_kernel(xr_ref, gr_ref, cw_ref, cb_ref, wa_ref, ba_ref, wx_ref, bx_ref, lam_ref,
                       y_ref, hl_ref, xpad, a_scr, u_scr, hs_scr, h_scr):
    tm = xr_ref.shape[0]

    @pl.when(pl.program_id(1) == 0)
    def _():
        xpad[0:8, :] = jnp.zeros((8, D_RNN), F32)
        h_scr[...] = jnp.zeros_like(h_scr)

    xr = xr_ref[...]
    xpad[8:8 + tm, :] = xr
    cw = cw_ref[...]
    xc = cb_ref[...] + (xpad[5:5 + tm, :] * cw[0:1] + xpad[6:6 + tm, :] * cw[1:2]
                        + xpad[7:7 + tm, :] * cw[2:3] + xr * cw[3:4])
    xpad[0:8, :] = xr[tm - 8:tm, :]
    a, u = _lru_gates(xc, wa_ref, ba_ref[...], wx_ref, bx_ref[...], lam_ref[...])
    a_scr[...] = a
    u_scr[...] = u

    def body(g, h):
        r0 = pl.multiple_of(g * 8, 8)
        a8 = a_scr[pl.ds(r0, 8), :]
        u8 = u_scr[pl.ds(r0, 8), :]
        rows = []
        for r in range(8):
            h = a8[r:r + 1, :] * h + u8[r:r + 1, :]
            rows.append(h)
        hs_scr[pl.ds(r0, 8), :] = jnp.concatenate(rows, axis=0)
        return h

    h = lax.fori_loop(0, tm // 8, body, h_scr[0:1, :])
    h_scr[0:1, :] = h
    y_ref[...] = (hs_scr[...] * _gelu(gr_ref[...])).astype(BF16)
    hl_ref[...] = h.reshape(1, 1, D_RNN)


def lru_prompt(p, batch, seq, lw, tm=256):
    nc = seq // tm
    full = lambda shape: pl.BlockSpec(shape, lambda b, c: (0,) * len(shape))
    y, hl = pl.pallas_call(
        _lru_prompt_kernel,
        grid=(batch, nc),
        in_specs=[pl.BlockSpec((tm, D_RNN), lambda b, c: (b * nc + c, OFF_XR // D_RNN)),
                  pl.BlockSpec((tm, D_RNN), lambda b, c: (b * nc + c, OFF_GR // D_RNN)),
                  full((CONV_W, D_RNN)), full((1, D_RNN)),
                  full(lw["wa"].shape), full((1, D_RNN)), full(lw["wx"].shape), full((1, D_RNN)),
                  full((1, D_RNN))],
        out_specs=[pl.BlockSpec((tm, D_RNN), lambda b, c: (b * nc + c, 0)),
                   pl.BlockSpec((1, 1, D_RNN), lambda b, c: (b, 0, 0))],
        out_shape=[jax.ShapeDtypeStruct((batch * seq, D_RNN), BF16),
                   jax.ShapeDtypeStruct((batch, 1, D_RNN), F32)],
        scratch_shapes=[pltpu.VMEM((tm + 8, D_RNN), F32), pltpu.VMEM((tm, D_RNN), F32),
                        pltpu.VMEM((tm, D_RNN), F32), pltpu.VMEM((tm, D_RNN), F32),
                        pltpu.VMEM((8, D_RNN), F32)],
        compiler_params=_params(2),
        name="lru_prompt",
    )(p, p, lw["cw"], lw["cb"], lw["wa"], lw["ba"], lw["wx"], lw["bx"], lw["lam"])
    return y, hl.reshape(batch, D_RNN)


def _lru_sample_kernel(xs_ref, gs_ref, c0_ref, h0_ref, cw_ref, cb_ref, wa_ref, ba_ref, wx_ref, bx_ref, lam_ref,
                       y_ref, hl_ref):
    steps = xs_ref.shape[0]
    cw = cw_ref[...]
    xcat = [c0_ref[k] for k in range(CONV_W - 1)] + [xs_ref[s] for s in range(steps)]
    h = h0_ref[...]
    for s in range(steps):
        xc = cb_ref[...] + (xcat[s] * cw[0:1] + xcat[s + 1] * cw[1:2] + xcat[s + 2] * cw[2:3]
                            + xcat[s + 3] * cw[3:4])
        a, u = _lru_gates(xc, wa_ref, ba_ref[...], wx_ref, bx_ref[...], lam_ref[...])
        h = a * h + u
        y_ref[s] = (h * _gelu(gs_ref[s])).astype(BF16)
    hl_ref[...] = h


def lru_sample(xs, gs, c0, h0, lw):
    steps, batch, _ = xs.shape
    return pl.pallas_call(
        _lru_sample_kernel,
        out_shape=[jax.ShapeDtypeStruct((steps, batch, D_RNN), BF16),
                   jax.ShapeDtypeStruct((batch, D_RNN), F32)],
        compiler_params=pltpu.CompilerParams(vmem_limit_bytes=VMEM_LIMIT),
        name="lru_sample",
    )(xs, gs, c0, h0, lw["cw"], lw["cb"], lw["wa"], lw["ba"], lw["wx"], lw["bx"], lw["lam"])


def _rope(x, cos, sin_signed):
    lane = lax.broadcasted_iota(jnp.int32, x.shape, 1)
    half = ROPE_DIM // 2
    rot = jnp.where(lane < half, pltpu.roll(x, LANES - half, 1), pltpu.roll(x, half, 1))
    return x * cos + rot * sin_signed


def _mla_prep_kernel(qa_ref, ckv_ref, kr_ref, cos_ref, sin_ref, gq_ref, gkv_ref, wn_ref, wp_ref, wkn_ref,
                     q_ref, kcat_ref, ckv_out, kr_out):
    cos = cos_ref[...]
    sin = sin_ref[...]
    qn = _rms(qa_ref[...], gq_ref[...]).astype(BF16)
    q_nope = _dot(qn, wn_ref[...]).astype(BF16)
    q_pe = _dot(qn, wp_ref[...])
    for h in range(MLA_HEADS):
        q_abs = _dot(q_nope[:, NOPE_DIM * h:NOPE_DIM * (h + 1)], wkn_ref[h])
        q_ref[h, :, 0:KV_LORA] = q_abs.astype(BF16)
        q_ref[h, :, KV_LORA:QK_PAD] = _rope(q_pe[:, LANES * h:LANES * (h + 1)], cos, sin).astype(BF16)
    ckv = _rms(ckv_ref[...], gkv_ref[...])
    kr = _rope(kr_ref[...], cos, sin)
    ckv_out[...] = ckv
    kr_out[...] = kr
    kcat_ref[:, 0:KV_LORA] = ckv.astype(BF16)
    kcat_ref[:, KV_LORA:QK_PAD] = kr.astype(BF16)


def mla_prep(p, cos, sin, mw, tm):
    t = p.shape[0]
    full = lambda shape: pl.BlockSpec(shape, lambda i: (0,) * len(shape))
    return pl.pallas_call(
        _mla_prep_kernel,
        grid=(t // tm,),
        in_specs=[pl.BlockSpec((tm, Q_LORA), lambda i: (i, OFF_QA // Q_LORA)),
                  pl.BlockSpec((tm, KV_LORA), lambda i: (i, OFF_CKV // KV_LORA)),
                  pl.BlockSpec((tm, LANES), lambda i: (i, OFF_KR // LANES)),
                  pl.BlockSpec((tm, LANES), lambda i: (i, 0)),
                  pl.BlockSpec((tm, LANES), lambda i: (i, 0)),
                  full((1, Q_LORA)), full((1, KV_LORA)),
                  full(mw["wn"].shape), full(mw["wp"].shape), full(mw["wkn"].shape)],
        out_specs=[pl.BlockSpec((MLA_HEADS, tm, QK_PAD), lambda i: (0, i, 0)),
                   pl.BlockSpec((tm, QK_PAD), lambda i: (i, 0)),
                   pl.BlockSpec((tm, KV_LORA), lambda i: (i, 0)),
                   pl.BlockSpec((tm, LANES), lambda i: (i, 0))],
        out_shape=[jax.ShapeDtypeStruct((MLA_HEADS, t, QK_PAD), BF16),
                   jax.ShapeDtypeStruct((t, QK_PAD), BF16),
                   jax.ShapeDtypeStruct((t, KV_LORA), F32),
                   jax.ShapeDtypeStruct((t, LANES), F32)],
        compiler_params=_params(1),
        name="mla_prep",
    )(p, p, p, cos, sin, mw["gq"], mw["gkv"], mw["wn"], mw["wp"], mw["wkn"])


def _attn_prompt_kernel(q_ref, k_ref, wv_ref, o_ref, m_scr, l_scr, acc_scr):
    qi = pl.program_id(1)
    ki = pl.program_id(2)
    heads, tq, _ = q_ref.shape
    tk = k_ref.shape[0]
    rows = heads * tq

    @pl.when(ki == 0)
    def _():
        m_scr[...] = jnp.full_like(m_scr, NEG_INF)
        l_scr[...] = jnp.zeros_like(l_scr)
        acc_scr[...] = jnp.zeros_like(acc_scr)

    @pl.when(ki <= qi)
    def _():
        q = q_ref[...].reshape(rows, QK_PAD)
        k = k_ref[...]
        s = _dot_nt(q, k) * ATTN_SCALE
        q_pos = qi * tq + lax.broadcasted_iota(jnp.int32, (heads, tq, tk), 1).reshape(rows, tk)
        k_pos = ki * tk + lax.broadcasted_iota(jnp.int32, (rows, tk), 1)
        s = jnp.where(k_pos <= q_pos, s, NEG_INF)
        m_old = m_scr[...]
        m_new = jnp.maximum(m_old, jnp.max(s, axis=-1, keepdims=True))
        alpha = jnp.exp(m_old - m_new)
        p = jnp.exp(s - m_new)
        l_scr[...] = alpha * l_scr[...] + jnp.sum(p, axis=-1, keepdims=True)
        acc_scr[...] = alpha * acc_scr[...] + _dot(p.astype(BF16), k[:, 0:KV_LORA])
        m_scr[...] = m_new

    @pl.when(ki == qi)
    def _():
        o = (acc_scr[...] / l_scr[...]).astype(BF16)
        for h in range(heads):
            o_ref[:, V_DIM * h:V_DIM * (h + 1)] = _dot(o[h * tq:(h + 1) * tq, :], wv_ref[h]).astype(BF16)


def attn_prompt(q, kcat, wv, batch, seq, tq=256):
    nq = seq // tq
    rows = MLA_HEADS * tq
    return pl.pallas_call(
        _attn_prompt_kernel,
        grid=(batch, nq, nq),
        in_specs=[pl.BlockSpec((MLA_HEADS, tq, QK_PAD), lambda b, i, j: (0, b * nq + i, 0)),
                  pl.BlockSpec((tq, QK_PAD), lambda b, i, j: (b * nq + jnp.minimum(i, j), 0)),
                  pl.BlockSpec(wv.shape, lambda b, i, j: (0, 0, 0))],
        out_specs=pl.BlockSpec((tq, MLA_HEADS * V_DIM), lambda b, i, j: (b * nq + i, 0)),
        out_shape=jax.ShapeDtypeStruct((batch * seq, MLA_HEADS * V_DIM), BF16),
        scratch_shapes=[pltpu.VMEM((rows, 1), F32), pltpu.VMEM((rows, 1), F32),
                        pltpu.VMEM((rows, KV_LORA), F32)],
        compiler_params=_params(3),
        name="attn_prompt",
    )(q, kcat, wv)


PAGES_PER_CHUNK = 8


def _attn_sample_kernel(pt_ref, q_ref, kn_ref, ckv_hbm, kr_hbm, o_ref, ckv_buf, kr_buf, sem):
    b = pl.program_id(0)
    nb = pl.num_programs(0)
    n_pages = ckv_buf.shape[1]
    slot = b % 2

    def page_copies(bb, sl):
        out = []
        for pg in range(n_pages):
            page = pt_ref[bb, pg]
            out.append(pltpu.make_async_copy(ckv_hbm.at[page], ckv_buf.at[sl, pg], sem.at[0, sl]))
            out.append(pltpu.make_async_copy(kr_hbm.at[page], kr_buf.at[sl, pg], sem.at[1, sl]))
        return out

    @pl.when(b == 0)
    def _():
        for c in page_copies(0, 0):
            c.start()

    @pl.when(b + 1 < nb)
    def _():
        for c in page_copies(b + 1, 1 - slot):
            c.start()

    for c in page_copies(b, slot):
        c.wait()

    q = q_ref[...]
    rows = q.shape[0]
    qa = q[:, 0:KV_LORA]
    qp = q[:, KV_LORA:KV_LORA + ROPE_DIM]
    m = jnp.full((rows, 1), NEG_INF, F32)
    l = jnp.zeros((rows, 1), F32)
    acc = jnp.zeros((rows, KV_LORA), F32)

    def online(s, v, m, l, acc):
        m_new = jnp.maximum(m, jnp.max(s, axis=-1, keepdims=True))
        alpha = jnp.exp(m - m_new)
        p = jnp.exp(s - m_new)
        l = alpha * l + jnp.sum(p, axis=-1, keepdims=True)
        acc = alpha * acc + _dot(p.astype(BF16), v)
        return m_new, l, acc

    ck = PAGES_PER_CHUNK
    for c in range(n_pages // ck):
        kc = ckv_buf[slot, ck * c:ck * (c + 1)].reshape(ck * PAGE_SIZE, KV_LORA).astype(BF16)
        kr = kr_buf[slot, ck * c:ck * (c + 1)].reshape(ck * PAGE_SIZE, ROPE_DIM).astype(BF16)
        s = (_dot_nt(qa, kc) + _dot_nt(qp, kr)) * ATTN_SCALE
        m, l, acc = online(s, kc, m, l, acc)

    kn = kn_ref[...]
    s = _dot_nt(q, kn) * ATTN_SCALE
    steps = rows // MLA_HEADS
    q_step = lax.broadcasted_iota(jnp.int32, (MLA_HEADS, steps, kn.shape[0]), 1).reshape(rows, kn.shape[0])
    k_step = lax.broadcasted_iota(jnp.int32, (rows, kn.shape[0]), 1)
    s = jnp.where(k_step <= q_step, s, NEG_INF)
    m, l, acc = online(s, kn[:, 0:KV_LORA], m, l, acc)
    o_ref[...] = acc / l


def attn_sample(page_table, q, kn, cache_ckv, cache_kr):
    nb, rows, _ = q.shape
    n_pages = page_table.shape[1]
    grid_spec = pltpu.PrefetchScalarGridSpec(
        num_scalar_prefetch=1,
        grid=(nb,),
        in_specs=[pl.BlockSpec((None, rows, QK_PAD), lambda b, pt: (b, 0, 0)),
                  pl.BlockSpec((None, kn.shape[1], QK_PAD), lambda b, pt: (b, 0, 0)),
                  pl.BlockSpec(memory_space=pl.ANY),
                  pl.BlockSpec(memory_space=pl.ANY)],
        out_specs=pl.BlockSpec((None, rows, KV_LORA), lambda b, pt: (b, 0, 0)),
        scratch_shapes=[pltpu.VMEM((2, n_pages, PAGE_SIZE, KV_LORA), F32),
                        pltpu.VMEM((2, n_pages, PAGE_SIZE, ROPE_DIM), F32),
                        pltpu.SemaphoreType.DMA((2, 2))],
    )
    return pl.pallas_call(
        _attn_sample_kernel,
        grid_spec=grid_spec,
        out_shape=jax.ShapeDtypeStruct((nb, rows, KV_LORA), F32),
        compiler_params=_params(1),
        name="attn_sample",
    )(page_table, q, kn, cache_ckv, cache_kr)


def _vproj_kernel(o_ref, wv_ref, out_ref):
    out_ref[...] = _dot(o_ref[...], wv_ref[...]).astype(BF16)


def vproj(o_lat, wv):
    heads, t, _ = o_lat.shape
    return pl.pallas_call(
        _vproj_kernel,
        grid=(heads,),
        in_specs=[pl.BlockSpec((None, t, KV_LORA), lambda h: (h, 0, 0)),
                  pl.BlockSpec((None, KV_LORA, V_DIM), lambda h: (h, 0, 0))],
        out_specs=pl.BlockSpec((t, V_DIM), lambda h: (0, h)),
        out_shape=jax.ShapeDtypeStruct((t, heads * V_DIM), BF16),
        compiler_params=_params(1),
        name="vproj",
    )(o_lat, wv)


def _mem_attn_kernel(q_ref, k_ref, v_ref, o_ref):
    q = q_ref[...].astype(BF16)
    k = k_ref[...].astype(BF16)
    v = v_ref[...].astype(BF16)
    for h in range(MEM_HEADS):
        sl = slice(MEM_HEAD_DIM * h, MEM_HEAD_DIM * (h + 1))
        s = _dot_nt(q[:, sl], k[:, sl]) * MEM_SCALE
        e = jnp.exp(s - jnp.max(s, axis=-1, keepdims=True))
        p = e / jnp.sum(e, axis=-1, keepdims=True)
        o_ref[:, sl] = _dot(p.astype(BF16), v[:, sl]).astype(BF16)


def mem_attn(q, q_col_block, mem_k, mem_v, rows_per_batch, tm):
    batch = mem_k.shape[0]
    width = MEM_HEADS * MEM_HEAD_DIM
    nc = rows_per_batch // tm
    return pl.pallas_call(
        _mem_attn_kernel,
        grid=(batch, nc),
        in_specs=[pl.BlockSpec((tm, width), lambda b, c: (b * nc + c, q_col_block)),
                  pl.BlockSpec((None, MEM_TOKENS, width), lambda b, c: (b, 0, 0)),
                  pl.BlockSpec((None, MEM_TOKENS, width), lambda b, c: (b, 0, 0))],
        out_specs=pl.BlockSpec((tm, width), lambda b, c: (b * nc + c, 0)),
        out_shape=jax.ShapeDtypeStruct((batch * rows_per_batch, width), BF16),
        compiler_params=_params(2),
        name="mem_attn",
    )(q, mem_k, mem_v)


def _merge_kernel(x_ref, gl_ref, ga_ref, gm_ref, yl_ref, ya_ref, ym_ref, wl_ref, wa_ref, wm_ref, wo_ref,
                  o_ref, acc):
    j = pl.program_id(1)

    @pl.when(j == 0)
    def _():
        acc[...] = jnp.zeros_like(acc)

    merged = (jax.nn.sigmoid(gl_ref[...]) * _dot(yl_ref[...], wl_ref[...])
              + jax.nn.sigmoid(ga_ref[...]) * _dot(ya_ref[...], wa_ref[...])
              + jax.nn.sigmoid(gm_ref[...]) * _dot(ym_ref[...], wm_ref[...]))
    acc[...] += _dot(merged.astype(BF16), wo_ref[...])

    @pl.when(j == pl.num_programs(1) - 1)
    def _():
        o_ref[...] = x_ref[...] + acc[...]


def merge(x, p, y_lru, o_mla, o_mem, ow, tm, tn=512):
    t = x.shape[0]
    nj = D_MODEL // tn
    gate = lambda k: pl.BlockSpec((tm, tn), lambda i, j: (i, (OFF_GATES + k * D_MODEL) // tn + j))
    act = lambda a: pl.BlockSpec((tm, a.shape[1]), lambda i, j: (i, 0))
    wcol = lambda w: pl.BlockSpec((w.shape[0], tn), lambda i, j: (0, j))
    return pl.pallas_call(
        _merge_kernel,
        grid=(t // tm, nj),
        in_specs=[pl.BlockSpec((tm, D_MODEL), lambda i, j: (i, 0)), gate(0), gate(1), gate(2),
                  act(y_lru), act(o_mla), act(o_mem),
                  wcol(ow["wl"]), wcol(ow["wa"]), wcol(ow["wm"]),
                  pl.BlockSpec((tn, D_MODEL), lambda i, j: (j, 0))],
        out_specs=pl.BlockSpec((tm, D_MODEL), lambda i, j: (i, 0)),
        out_shape=jax.ShapeDtypeStruct((t, D_MODEL), F32),
        scratch_shapes=[pltpu.VMEM((tm, D_MODEL), F32)],
        compiler_params=_params(2),
        name="merge",
    )(x, p, p, p, y_lru, o_mla, o_mem, ow["wl"], ow["wa"], ow["wm"], ow["wo"])


N_TOP = PEER_TOPK + 1
CAND_PAIRS = [(i, j) for i in range(N_TOP) for j in range(N_TOP) if (i + 1) * (j + 1) <= N_TOP]
CAND_ROWS = -(-len(CAND_PAIRS) // 8) * 8
TOP_ROWS = -(-N_TOP // 8) * 8


def _top_values(s, n, out_ref):
    for r in range(n):
        m = jnp.max(s, axis=0, keepdims=True)
        out_ref[r:r + 1, :] = m
        s = jnp.where(s == m, NEG_INF, s)


def _peer_route_kernel(x_ref, g_ref, wq_ref, k_ref, ht_ref, thr_ref, cf_ref, s2_ref, p2_ref,
                       hb, top_a, top_b, cand, top_c):
    @pl.when(pl.program_id(1) == 0)
    def _():
        hf = _rms(x_ref[...], g_ref[...])
        hb[...] = hf.astype(BF16)
        ht_ref[...] = hf.T.astype(BF16)

    qb = _dot(hb[...], wq_ref[...]).astype(BF16)
    s1 = _dot_nt(k_ref[0], qb[:, 0:PEER_KEYS])
    s2 = _dot_nt(k_ref[1], qb[:, PEER_KEYS:2 * PEER_KEYS])
    _top_values(s1, N_TOP, top_a)
    _top_values(s2, N_TOP, top_b)
    cand[...] = jnp.full(cand.shape, NEG_INF, F32)
    for n, (i, j) in enumerate(CAND_PAIRS):
        cand[n:n + 1, :] = top_a[i:i + 1, :] + top_b[j:j + 1, :]
    _top_values(cand[...], N_TOP, top_c)
    c = top_c[...]
    c_max = c[0:1, :]
    z = jnp.sum(jnp.exp(c[0:PEER_TOPK, :] - c_max), axis=0, keepdims=True)
    tau = 0.5 * (c[PEER_TOPK - 1:PEER_TOPK, :] + c[PEER_TOPK:PEER_TOPK + 1, :])
    thr_ref[...] = tau - s1
    cf_ref[...] = jnp.exp(s1 - top_a[0:1, :]) / z
    s2_ref[...] = s2
    p2_ref[...] = jnp.exp(s2 - top_b[0:1, :])


def peer_route(x, g, wq, keys, tm):
    t = x.shape[0]
    stat = pl.BlockSpec((None, PEER_KEYS, tm), lambda i, h: (h, 0, i))
    stat_shape = jax.ShapeDtypeStruct((PEER_HEADS, PEER_KEYS, t), F32)
    return pl.pallas_call(
        _peer_route_kernel,
        grid=(t // tm, PEER_HEADS),
        in_specs=[pl.BlockSpec((tm, D_MODEL), lambda i, h: (i, 0)),
                  pl.BlockSpec((1, D_MODEL), lambda i, h: (0, 0)),
                  pl.BlockSpec((D_MODEL, 2 * PEER_KEYS), lambda i, h: (0, h)),
                  pl.BlockSpec((None, 2, PEER_KEYS, PEER_KEYS), lambda i, h: (h, 0, 0, 0))],
        out_specs=[pl.BlockSpec((D_MODEL, tm), lambda i, h: (0, i)), stat, stat, stat, stat],
        out_shape=[jax.ShapeDtypeStruct((D_MODEL, t), BF16), stat_shape, stat_shape, stat_shape, stat_shape],
        scratch_shapes=[pltpu.VMEM((tm, D_MODEL), BF16), pltpu.VMEM((TOP_ROWS, tm), F32),
                        pltpu.VMEM((TOP_ROWS, tm), F32), pltpu.VMEM((CAND_ROWS, tm), F32),
                        pltpu.VMEM((TOP_ROWS, tm), F32)],
        compiler_params=_params(2),
        name="peer_route",
    )(x, g.reshape(1, D_MODEL), wq, keys)


def _peer_mix_kernel(ht_ref, u_ref, v_ref, thr_ref, cf_ref, s2_ref, p2_ref, x_ref, g_ref, y_ref, acc, c_scr):
    j = pl.program_id(1)
    ec = u_ref.shape[0]

    @pl.when(j == 0)
    def _():
        acc[...] = jnp.zeros_like(acc)

    for el in range(ec // PEER_KEYS):
        e1 = j * (ec // PEER_KEYS) + el
        c = jnp.zeros(c_scr.shape[1:], F32)
        for h in range(PEER_HEADS):
            thr = thr_ref[h, pl.ds(e1, 1), :]
            cf = cf_ref[h, pl.ds(e1, 1), :]
            c = c + jnp.where(s2_ref[h] > thr, p2_ref[h] * cf, 0.0)
        c_scr[el] = c

    act = _dot(u_ref[...], ht_ref[...])
    coef = (c_scr[...].reshape(act.shape) * _gelu(act)).astype(BF16)
    acc[...] += _dot_tn(coef, v_ref[...])

    @pl.when(j == pl.num_programs(1) - 1)
    def _():
        y_ref[...] = _rms(x_ref[...] + acc[...], g_ref[...])


def peer_mix(ht, u, v, thr, cf, s2, p2, x, g, tm, ec=512):
    t = x.shape[0]
    stat = pl.BlockSpec((PEER_HEADS, PEER_KEYS, tm), lambda i, j: (0, 0, i))
    return pl.pallas_call(
        _peer_mix_kernel,
        grid=(t // tm, PEER_EXPERTS // ec),
        in_specs=[pl.BlockSpec((D_MODEL, tm), lambda i, j: (0, i)),
                  pl.BlockSpec((ec, D_MODEL), lambda i, j: (j, 0)),
                  pl.BlockSpec((ec, D_MODEL), lambda i, j: (j, 0)),
                  stat, stat, stat, stat,
                  pl.BlockSpec((tm, D_MODEL), lambda i, j: (i, 0)),
                  pl.BlockSpec((1, D_MODEL), lambda i, j: (0, 0))],
        out_specs=pl.BlockSpec((tm, D_MODEL), lambda i, j: (i, 0)),
        out_shape=jax.ShapeDtypeStruct((t, D_MODEL), F32),
        scratch_shapes=[pltpu.VMEM((tm, D_MODEL), F32), pltpu.VMEM((ec // PEER_KEYS, PEER_KEYS, tm), F32)],
        compiler_params=_params(2),
        name="peer_mix",
    )(ht, u, v, thr, cf, s2, p2, x, g.reshape(1, D_MODEL))


def _block_diag(w, per_tile):
    nb, bw, _ = w.shape
    w4 = w.reshape(nb // per_tile, per_tile, bw, bw)
    eye = jnp.eye(per_tile, dtype=w.dtype)
    tiles = w4[:, :, :, None, :] * eye[None, :, None, :, None]
    return tiles.reshape(nb // per_tile, per_tile * bw, per_tile * bw).astype(BF16)


def _rope_tables(pos):
    half = ROPE_DIM // 2
    inv = ROPE_THETA ** (-jnp.arange(half, dtype=F32) / half)
    ang = pos.astype(F32)[:, None] * inv[None, :]
    cos, sin = jnp.cos(ang), jnp.sin(ang)
    pad = jnp.zeros((pos.shape[0], LANES - ROPE_DIM), F32)
    return (jnp.concatenate([cos, cos, pad], axis=1), jnp.concatenate([-sin, sin, pad], axis=1))


def _prepare_weights(w_in, conv_w, conv_b, w_rec_a, b_rec_a, w_rec_x, b_rec_x, lru_lambda, w_o_lru,
                     g_q_a, w_q_b, g_kv_a, w_kv_b, w_o_mla, w_mem_k, w_mem_v, w_o_mem, w_out,
                     w_peer_query, peer_sub_keys, peer_u, peer_v):
    offs = np.cumsum([D_RNN, D_RNN, Q_LORA, KV_LORA + ROPE_DIM, MEM_HEADS * MEM_HEAD_DIM]).tolist()
    w_xr, w_gr, w_qa, w_kva, w_qm, w_gates = jnp.split(w_in, offs, axis=1)
    pad = jnp.zeros((D_MODEL, P_WIDTH - OFF_KR - ROPE_DIM), w_in.dtype)
    w_p = jnp.concatenate([w_gates, w_xr, w_gr, w_qm, w_qa, w_kva, pad], axis=1).astype(BF16)
    row = lambda v: v.reshape(1, -1)
    lw = dict(cw=conv_w, cb=row(conv_b), wa=_block_diag(w_rec_a, 4), ba=row(b_rec_a),
              wx=_block_diag(w_rec_x, 4), bx=row(b_rec_x), lam=row(lru_lambda))
    w_pe = jnp.pad(w_q_b[:, :, NOPE_DIM:], ((0, 0), (0, 0), (0, LANES - ROPE_DIM)))
    mw = dict(gq=row(g_q_a), gkv=row(g_kv_a),
              wn=w_q_b[:, :, :NOPE_DIM].reshape(Q_LORA, MLA_HEADS * NOPE_DIM).astype(BF16),
              wp=w_pe.reshape(Q_LORA, MLA_HEADS * LANES).astype(BF16),
              wkn=jnp.transpose(w_kv_b[:, :, :NOPE_DIM], (1, 2, 0)).astype(BF16),
              wv=jnp.transpose(w_kv_b[:, :, NOPE_DIM:], (1, 0, 2)).astype(BF16))
    ow = dict(wl=w_o_lru.astype(BF16), wa=w_o_mla.astype(BF16), wm=w_o_mem.astype(BF16), wo=w_out.astype(BF16))
    w_mem = jnp.concatenate([w_mem_k.reshape(D_MODEL, -1), w_mem_v.reshape(D_MODEL, -1)], axis=1).astype(BF16)
    pw = dict(wq=w_peer_query.reshape(D_MODEL, -1).astype(BF16), keys=peer_sub_keys.astype(BF16),
              u=peer_u.astype(BF16), v=peer_v.astype(BF16))
    return w_p, lw, mw, ow, w_mem, pw


def _peer_and_norm(x2, g_ffn, g_final, pw, tm):
    ht, thr, cf, s2, p2 = peer_route(x2, g_ffn, pw["wq"], pw["keys"], tm)
    return peer_mix(ht, pw["u"], pw["v"], thr, cf, s2, p2, x2, g_final, tm)


def kernel(x_prompt, x_sample, cache_mla_ckv, cache_mla_krope, cache_mem_k, cache_mem_v, state_lru_h, state_lru_conv, page_table, mem_prompt, g_mix, w_in, conv_w, conv_b, w_rec_a, b_rec_a, w_rec_x, b_rec_x, lru_lambda, w_o_lru, g_q_a, w_q_b, g_kv_a, w_kv_b, w_o_mla, g_mem, w_mem_k, w_mem_v, w_o_mem, w_out, g_ffn, w_peer_query, peer_sub_keys, peer_u, peer_v, g_final):
    batch, seq, _ = x_prompt.shape
    dec_batch, dec_seq, _ = x_sample.shape
    past_len = page_table.shape[1] * PAGE_SIZE
    tp, ts = batch * seq, dec_batch * dec_seq
    tm = 512
    mem_width = MEM_HEADS * MEM_HEAD_DIM

    w_p, lw, mw, ow, w_mem, pw = _prepare_weights(
        w_in, conv_w, conv_b, w_rec_a, b_rec_a, w_rec_x, b_rec_x, lru_lambda, w_o_lru, g_q_a, w_q_b, g_kv_a,
        w_kv_b, w_o_mla, w_mem_k, w_mem_v, w_o_mem, w_out, w_peer_query, peer_sub_keys, peer_u, peer_v)

    xp = x_prompt.reshape(tp, D_MODEL)
    pp = norm_matmul(xp, g_mix, w_p, tm, 512, "in_proj")
    mem_kv = norm_matmul(mem_prompt.reshape(batch * MEM_TOKENS, D_MODEL), g_mem, w_mem, tm, 512, "mem_kv")
    mem_k_p = mem_kv[:, :mem_width].reshape(batch, MEM_TOKENS, mem_width)
    mem_v_p = mem_kv[:, mem_width:].reshape(batch, MEM_TOKENS, mem_width)
    y_lru_p, lru_h_p = lru_prompt(pp, batch, seq, lw)
    cos_p, sin_p = _rope_tables(jnp.tile(jnp.arange(seq), batch))
    q_p, kcat_p, ckv_p, kr_p = mla_prep(pp, cos_p, sin_p, mw, tm)
    o_mla_p = attn_prompt(q_p, kcat_p, mw["wv"], batch, seq)
    o_mem_p = mem_attn(pp, OFF_QM // mem_width, mem_k_p, mem_v_p, seq, tm)
    x2_p = merge(xp, pp, y_lru_p, o_mla_p, o_mem_p, ow, tm)
    y_p = _peer_and_norm(x2_p, g_ffn, g_final, pw, tm)

    xs = x_sample.reshape(ts, D_MODEL)
    ps = norm_matmul(xs, g_mix, w_p, tm, 512, "in_proj")
    step_major = lambda a: jnp.transpose(a.reshape(dec_batch, dec_seq, -1), (1, 0, 2))
    xr_s = ps[:, OFF_XR:OFF_XR + D_RNN]
    y_lru_s, lru_h_s = lru_sample(step_major(xr_s), step_major(ps[:, OFF_GR:OFF_GR + D_RNN]),
                                  jnp.transpose(state_lru_conv, (1, 0, 2)), state_lru_h, lw)
    y_lru_s = jnp.transpose(y_lru_s, (1, 0, 2)).reshape(ts, D_RNN)
    cos_s, sin_s = _rope_tables(jnp.tile(past_len + jnp.arange(dec_seq), dec_batch))
    q_s, kcat_s, ckv_s, kr_s = mla_prep(ps, cos_s, sin_s, mw, tm)
    q_rows = jnp.transpose(q_s.reshape(MLA_HEADS, dec_batch, dec_seq, QK_PAD), (1, 0, 2, 3))
    q_rows = q_rows.reshape(dec_batch, MLA_HEADS * dec_seq, QK_PAD)
    k_new = jnp.pad(kcat_s.reshape(dec_batch, dec_seq, QK_PAD), ((0, 0), (0, PAGE_SIZE - dec_seq), (0, 0)))
    o_lat = attn_sample(page_table, q_rows, k_new, cache_mla_ckv, cache_mla_krope)
    o_lat = jnp.transpose(o_lat.reshape(dec_batch, MLA_HEADS, dec_seq, KV_LORA), (1, 0, 2, 3))
    o_mla_s = vproj(o_lat.reshape(MLA_HEADS, ts, KV_LORA).astype(BF16), mw["wv"])
    q_pad_rows = 16
    qm_s = jnp.pad(ps[:, OFF_QM:OFF_QM + mem_width].reshape(dec_batch, dec_seq, mem_width),
                   ((0, 0), (0, q_pad_rows - dec_seq), (0, 0))).reshape(dec_batch * q_pad_rows, mem_width)
    o_mem_s = mem_attn(qm_s, 0, cache_mem_k.reshape(dec_batch, MEM_TOKENS, mem_width),
                       cache_mem_v.reshape(dec_batch, MEM_TOKENS, mem_width), q_pad_rows, q_pad_rows)
    o_mem_s = o_mem_s.reshape(dec_batch, q_pad_rows, mem_width)[:, :dec_seq].reshape(ts, mem_width)
    x2_s = merge(xs, ps, y_lru_s, o_mla_s, o_mem_s, ow, tm)
    y_s = _peer_and_norm(x2_s, g_ffn, g_final, pw, tm)

    head_shape = (MEM_TOKENS, MEM_HEADS, MEM_HEAD_DIM)
    conv_tail = CONV_W - 1
    return (y_p.reshape(batch, seq, D_MODEL), y_s.reshape(dec_batch, dec_seq, D_MODEL),
            ckv_p.reshape(batch, seq, KV_LORA), kr_p[:, :ROPE_DIM].reshape(batch, seq, ROPE_DIM),
            mem_k_p.reshape((batch,) + head_shape), mem_v_p.reshape((batch,) + head_shape),
            lru_h_p, pp[:, OFF_XR:OFF_XR + D_RNN].reshape(batch, seq, D_RNN)[:, seq - conv_tail:],
            ckv_s.reshape(dec_batch, dec_seq, KV_LORA), kr_s[:, :ROPE_DIM].reshape(dec_batch, dec_seq, ROPE_DIM),
            lru_h_s, xr_s.reshape(dec_batch, dec_seq, D_RNN)[:, dec_seq - conv_tail:])
```

```python
import functools
import math

import jax
import jax.numpy as jnp
import numpy as np
from jax import lax
from jax.experimental import pallas as pl
from jax.experimental.pallas import tpu as pltpu

F32 = jnp.float32
BF16 = jnp.bfloat16

D_MODEL = 2048
EPS = 1e-6
D_RNN = 1024
LRU_BLOCKS = 16
CONV_W = 4
LRU_C = 8.0
MLA_HEADS = 16
Q_LORA = 512
KV_LORA = 256
NOPE_DIM = 128
ROPE_DIM = 64
V_DIM = 128
ROPE_THETA = 10000.0
MEM_TOKENS = 256
MEM_HEADS = 4
MEM_HEAD_DIM = 256
PEER_HEADS = 8
PEER_KEYS = 128
PEER_EXPERTS = PEER_KEYS * PEER_KEYS
PEER_TOPK = 16
PAGE_SIZE = 128

LANES = 128
QK_PAD = KV_LORA + LANES
ATTN_SCALE = (NOPE_DIM + ROPE_DIM) ** -0.5
MEM_SCALE = MEM_HEAD_DIM ** -0.5
NEG_INF = float("-inf")

OFF_GATES = 0
OFF_XR = 3 * D_MODEL
OFF_GR = OFF_XR + D_RNN
OFF_QM = OFF_GR + D_RNN
OFF_QA = OFF_QM + MEM_HEADS * MEM_HEAD_DIM
OFF_CKV = OFF_QA + Q_LORA
OFF_KR = OFF_CKV + KV_LORA
P_WIDTH = OFF_KR + 2 * LANES
PROJ_TN = 2048

VMEM_LIMIT = 56 * 1024 * 1024


def _params(n_grid):
    return pltpu.CompilerParams(dimension_semantics=("arbitrary",) * n_grid, vmem_limit_bytes=VMEM_LIMIT)


def _rms(x, g):
    ms = jnp.mean(x * x, axis=-1, keepdims=True)
    return x * lax.rsqrt(ms + EPS) * g


def _gelu(x):
    cdf = 0.5 * (1.0 + jnp.tanh(math.sqrt(2.0 / math.pi) * (x + 0.044715 * (x * x * x))))
    return x * cdf


def _dot(a, b):
    return jnp.dot(a, b, preferred_element_type=F32)


def _dot_nt(a, b):
    return lax.dot_general(a, b, (((1,), (1,)), ((), ())), preferred_element_type=F32)


def _dot_tn(a, b):
    return lax.dot_general(a, b, (((0,), (0,)), ((), ())), preferred_element_type=F32)


def _norm_matmul_kernel(x_ref, g_ref, w_ref, o_ref, h_scr):
    @pl.when(pl.program_id(1) == 0)
    def _():
        h_scr[...] = _rms(x_ref[...], g_ref[...]).astype(BF16)

    o_ref[...] = _dot(h_scr[...], w_ref[...])


def norm_matmul(x, g, w, tm, tn, name):
    t, d = x.shape
    n = w.shape[1]
    return pl.pallas_call(
        _norm_matmul_kernel,
        grid=(t // tm, n // tn),
        in_specs=[pl.BlockSpec((tm, d), lambda i, j: (i, 0)),
                  pl.BlockSpec((1, d), lambda i, j: (0, 0)),
                  pl.BlockSpec((d, tn), lambda i, j: (0, j))],
        out_specs=pl.BlockSpec((tm, tn), lambda i, j: (i, j)),
        out_shape=jax.ShapeDtypeStruct((t, n), F32),
        scratch_shapes=[pltpu.VMEM((tm, d), BF16)],
        compiler_params=_params(2),
        name=name,
    )(x, g.reshape(1, d), w)


def _lru_gates(xc, wa_ref, ba, wx_ref, bx, lam):
    xb = xc.astype(BF16)
    nk = wa_ref.shape[0]
    wk = D_RNN // nk
    ra = [_dot(xb[:, wk * k:wk * (k + 1)], wa_ref[k]) for k in range(nk)]
    ix = [_dot(xb[:, wk * k:wk * (k + 1)], wx_ref[k]) for k in range(nk)]
    r = jax.nn.sigmoid(jnp.concatenate(ra, axis=1) + ba)
    i = jax.nn.sigmoid(jnp.concatenate(ix, axis=1) + bx)
    neg_lam = -lam
    softplus = jnp.maximum(neg_lam, 0.0) + jnp.log1p(jnp.exp(-jnp.abs(neg_lam)))
    log_a = -LRU_C * r * softplus
    a = jnp.exp(log_a)
    u = jnp.sqrt(-jnp.tanh(log_a) * (a * a + 1.0)) * (i * xc)
    return a, u


def _lru_prompt_kernel(xr_ref, gr_ref, cw_ref, cb_ref, wa_ref, ba_ref, wx_ref, bx_ref, lam_ref,
                       y_ref, hl_ref, xpad, a_scr, u_scr, hs_scr, h_scr):
    tm = xr_ref.shape[0]

    @pl.when(pl.program_id(1) == 0)
    def _():
        xpad[0:8, :] = jnp.zeros((8, D_RNN), F32)
        h_scr[...] = jnp.zeros_like(h_scr)

    xr = xr_ref[...]
    xpad[8:8 + tm, :] = xr
    cw = cw_ref[...]
    xc = cb_ref[...] + (xpad[5:5 + tm, :] * cw[0:1] + xpad[6:6 + tm, :] * cw[1:2]
                        + xpad[7:7 + tm, :] * cw[2:3] + xr * cw[3:4])
    xpad[0:8, :] = xr[tm - 8:tm, :]
    a, u = _lru_gates(xc, wa_ref, ba_ref[...], wx_ref, bx_ref[...], lam_ref[...])
    a_scr[...] = a
    u_scr[...] = u

    def body(g, h):
        r0 = pl.multiple_of(g * 8, 8)
        a8 = a_scr[pl.ds(r0, 8), :]
        u8 = u_scr[pl.ds(r0, 8), :]
        rows = []
        for r in range(8):
            h = a8[r:r + 1, :] * h + u8[r:r + 1, :]
            rows.append(h)
        hs_scr[pl.ds(r0, 8), :] = jnp.concatenate(rows, axis=0)
        return h

    h = lax.fori_loop(0, tm // 8, body, h_scr[0:1, :])
    h_scr[0:1, :] = h
    y_ref[...] = (hs_scr[...] * _gelu(gr_ref[...])).astype(BF16)
    hl_ref[...] = h.reshape(1, 1, D_RNN)


def lru_prompt(p, batch, seq, lw, tm=256):
    nc = seq // tm
    full = lambda shape: pl.BlockSpec(shape, lambda b, c: (0,) * len(shape))
    y, hl = pl.pallas_call(
        _lru_prompt_kernel,
        grid=(batch, nc),
        in_specs=[pl.BlockSpec((tm, D_RNN), lambda b, c: (b * nc + c, OFF_XR // D_RNN)),
                  pl.BlockSpec((tm, D_RNN), lambda b, c: (b * nc + c, OFF_GR // D_RNN)),
                  full((CONV_W, D_RNN)), full((1, D_RNN)),
                  full(lw["wa"].shape), full((1, D_RNN)), full(lw["wx"].shape), full((1, D_RNN)),
                  full((1, D_RNN))],
        out_specs=[pl.BlockSpec((tm, D_RNN), lambda b, c: (b * nc + c, 0)),
                   pl.BlockSpec((1, 1, D_RNN), lambda b, c: (b, 0, 0))],
        out_shape=[jax.ShapeDtypeStruct((batch * seq, D_RNN), BF16),
                   jax.ShapeDtypeStruct((batch, 1, D_RNN), F32)],
        scratch_shapes=[pltpu.VMEM((tm + 8, D_RNN), F32), pltpu.VMEM((tm, D_RNN), F32),
                        pltpu.VMEM((tm, D_RNN), F32), pltpu.VMEM((tm, D_RNN), F32),
                        pltpu.VMEM((8, D_RNN), F32)],
        compiler_params=_params(2),
        name="lru_prompt",
    )(p, p, lw["cw"], lw["cb"], lw["wa"], lw["ba"], lw["wx"], lw["bx"], lw["lam"])
    return y, hl.reshape(batch, D_RNN)


def _lru_sample_kernel(xs_ref, gs_ref, c0_ref, h0_ref, cw_ref, cb_ref, wa_ref, ba_ref, wx_ref, bx_ref, lam_ref,
                       y_ref, hl_ref):
    steps = xs_ref.shape[0]
    cw = cw_ref[...]
    xcat = [c0_ref[k] for k in range(CONV_W - 1)] + [xs_ref[s] for s in range(steps)]
    h = h0_ref[...]
    for s in range(steps):
        xc = cb_ref[...] + (xcat[s] * cw[0:1] + xcat[s + 1] * cw[1:2] + xcat[s + 2] * cw[2:3]
                            + xcat[s + 3] * cw[3:4])
        a, u = _lru_gates(xc, wa_ref, ba_ref[...], wx_ref, bx_ref[...], lam_ref[...])
        h = a * h + u
        y_ref[s] = (h * _gelu(gs_ref[s])).astype(BF16)
    hl_ref[...] = h


def lru_sample(xs, gs, c0, h0, lw):
    steps, batch, _ = xs.shape
    return pl.pallas_call(
        _lru_sample_kernel,
        out_shape=[jax.ShapeDtypeStruct((steps, batch, D_RNN), BF16),
                   jax.ShapeDtypeStruct((batch, D_RNN), F32)],
        compiler_params=pltpu.CompilerParams(vmem_limit_bytes=VMEM_LIMIT),
        name="lru_sample",
    )(xs, gs, c0, h0, lw["cw"], lw["cb"], lw["wa"], lw["ba"], lw["wx"], lw["bx"], lw["lam"])


def _rope(x, cos, sin_signed):
    lane = lax.broadcasted_iota(jnp.int32, x.shape, 1)
    half = ROPE_DIM // 2
    rot = jnp.where(lane < half, pltpu.roll(x, LANES - half, 1), pltpu.roll(x, half, 1))
    return x * cos + rot * sin_signed


def _mla_prep_kernel(qa_ref, ckv_ref, kr_ref, cos_ref, sin_ref, gq_ref, gkv_ref, wn_ref, wp_ref, wkn_ref,
                     q_ref, kcat_ref, ckv_out, kr_out):
    cos = cos_ref[...]
    sin = sin_ref[...]
    qn = _rms(qa_ref[...], gq_ref[...]).astype(BF16)
    q_nope = _dot(qn, wn_ref[...]).astype(BF16)
    q_pe = _dot(qn, wp_ref[...])
    for h in range(MLA_HEADS):
        q_abs = _dot(q_nope[:, NOPE_DIM * h:NOPE_DIM * (h + 1)], wkn_ref[h])
        q_ref[h, :, 0:KV_LORA] = q_abs.astype(BF16)
        q_ref[h, :, KV_LORA:QK_PAD] = _rope(q_pe[:, LANES * h:LANES * (h + 1)], cos, sin).astype(BF16)
    ckv = _rms(ckv_ref[...], gkv_ref[...])
    kr = _rope(kr_ref[...], cos, sin)
    ckv_out[...] = ckv
    kr_out[...] = kr
    kcat_ref[:, 0:KV_LORA] = ckv.astype(BF16)
    kcat_ref[:, KV_LORA:QK_PAD] = kr.astype(BF16)


def mla_prep(p, cos, sin, mw, tm):
    t = p.shape[0]
    full = lambda shape: pl.BlockSpec(shape, lambda i: (0,) * len(shape))
    return pl.pallas_call(
        _mla_prep_kernel,
        grid=(t // tm,),
        in_specs=[pl.BlockSpec((tm, Q_LORA), lambda i: (i, OFF_QA // Q_LORA)),
                  pl.BlockSpec((tm, KV_LORA), lambda i: (i, OFF_CKV // KV_LORA)),
                  pl.BlockSpec((tm, LANES), lambda i: (i, OFF_KR // LANES)),
                  pl.BlockSpec((tm, LANES), lambda i: (i, 0)),
                  pl.BlockSpec((tm, LANES), lambda i: (i, 0)),
                  full((1, Q_LORA)), full((1, KV_LORA)),
                  full(mw["wn"].shape), full(mw["wp"].shape), full(mw["wkn"].shape)],
        out_specs=[pl.BlockSpec((MLA_HEADS, tm, QK_PAD), lambda i: (0, i, 0)),
                   pl.BlockSpec((tm, QK_PAD), lambda i: (i, 0)),
                   pl.BlockSpec((tm, KV_LORA), lambda i: (i, 0)),
                   pl.BlockSpec((tm, LANES), lambda i: (i, 0))],
        out_shape=[jax.ShapeDtypeStruct((MLA_HEADS, t, QK_PAD), BF16),
                   jax.ShapeDtypeStruct((t, QK_PAD), BF16),
                   jax.ShapeDtypeStruct((t, KV_LORA), F32),
                   jax.ShapeDtypeStruct((t, LANES), F32)],
        compiler_params=_params(1),
        name="mla_prep",
    )(p, p, p, cos, sin, mw["gq"], mw["gkv"], mw["wn"], mw["wp"], mw["wkn"])


ATTN_HEAD_GROUP = 4

def _attn_prompt_kernel(q_ref, k_ref, wv_ref, o_ref, m_scr, l_scr, acc_scr):
    qi = pl.program_id(1)
    ki = pl.program_id(2)
    heads, tq, _ = q_ref.shape
    tk = k_ref.shape[0]

    @pl.when(ki == 0)
    def _():
        m_scr[...] = jnp.full_like(m_scr, NEG_INF)
        l_scr[...] = jnp.zeros_like(l_scr)
        acc_scr[...] = jnp.zeros_like(acc_scr)

    def lanes(x, width):
        return jnp.concatenate([x] * (width // LANES), axis=1)

    def block(masked):
        k = k_ref[...]
        v = k[:, 0:KV_LORA]
        g = ATTN_HEAD_GROUP
        if masked:
            col = lax.broadcasted_iota(jnp.int32, (g, tq, tk), 2).reshape(g * tq, tk)
            row = lax.broadcasted_iota(jnp.int32, (g, tq, tk), 1).reshape(g * tq, tk)
            visible = col <= row
        for hg in range(heads // g):
            rs = slice(hg * g * tq, (hg + 1) * g * tq)
            s = _dot_nt(q_ref[hg * g:(hg + 1) * g].reshape(g * tq, QK_PAD), k)
            if masked:
                s = jnp.where(visible, s, NEG_INF)
            m_old = m_scr[rs, :]
            m_new = jnp.maximum(m_old, jnp.max(s, axis=-1, keepdims=True))
            alpha = jnp.exp((m_old - m_new) * ATTN_SCALE)
            p = jnp.exp((s - lanes(m_new, tk)) * ATTN_SCALE)
            l_scr[rs, :] = alpha * l_scr[rs, :] + jnp.sum(p, axis=-1, keepdims=True)
            acc_scr[rs, :] = lanes(alpha, KV_LORA) * acc_scr[rs, :] + _dot(p.astype(BF16), v)
            m_scr[rs, :] = m_new

    @pl.when(ki < qi)
    def _():
        block(False)

    @pl.when(ki == qi)
    def _():
        block(True)
        for h in range(heads):
            rs = slice(h * tq, (h + 1) * tq)
            o = (acc_scr[rs, :] / lanes(l_scr[rs, :], KV_LORA)).astype(BF16)
            o_ref[:, V_DIM * h:V_DIM * (h + 1)] = _dot(o, wv_ref[h]).astype(BF16)


def attn_prompt(q, kcat, wv, batch, seq, tq=256):
    nq = seq // tq
    return pl.pallas_call(
        _attn_prompt_kernel,
        grid=(batch, nq, nq),
        in_specs=[pl.BlockSpec((MLA_HEADS, tq, QK_PAD), lambda b, i, j: (0, b * nq + i, 0)),
                  pl.BlockSpec((tq, QK_PAD), lambda b, i, j: (b * nq + jnp.minimum(i, j), 0)),
                  pl.BlockSpec(wv.shape, lambda b, i, j: (0, 0, 0))],
        out_specs=pl.BlockSpec((tq, MLA_HEADS * V_DIM), lambda b, i, j: (b * nq + i, 0)),
        out_shape=jax.ShapeDtypeStruct((batch * seq, MLA_HEADS * V_DIM), BF16),
        scratch_shapes=[pltpu.VMEM((MLA_HEADS * tq, LANES), F32), pltpu.VMEM((MLA_HEADS * tq, LANES), F32),
                        pltpu.VMEM((MLA_HEADS * tq, KV_LORA), F32)],
        compiler_params=_params(3),
        name="attn_prompt",
    )(q, kcat, wv)


PAGES_PER_CHUNK = 32


def _attn_sample_kernel(pt_ref, q_ref, kn_ref, ckv_hbm, kr_hbm, o_ref, ckv_buf, kr_buf, sem):
    b = pl.program_id(0)
    nb = pl.num_programs(0)
    n_pages = ckv_buf.shape[1]
    slot = b % 2

    def page_copies(bb, sl):
        out = []
        for pg in range(n_pages):
            page = pt_ref[bb, pg]
            out.append(pltpu.make_async_copy(ckv_hbm.at[page], ckv_buf.at[sl, pg], sem.at[0, sl]))
            out.append(pltpu.make_async_copy(kr_hbm.at[page], kr_buf.at[sl, pg], sem.at[1, sl]))
        return out

    @pl.when(b == 0)
    def _():
        for c in page_copies(0, 0):
            c.start()

    @pl.when(b + 1 < nb)
    def _():
        for c in page_copies(b + 1, 1 - slot):
            c.start()

    for c in page_copies(b, slot):
        c.wait()

    q = q_ref[...]
    rows = q.shape[0]
    qa = q[:, 0:KV_LORA]
    qp = q[:, KV_LORA:KV_LORA + ROPE_DIM]
    m = jnp.full((rows, 1), NEG_INF, F32)
    l = jnp.zeros((rows, 1), F32)
    acc = jnp.zeros((rows, KV_LORA), F32)

    def online(s, v, m, l, acc):
        m_new = jnp.maximum(m, jnp.max(s, axis=-1, keepdims=True))
        alpha = jnp.exp(m - m_new)
        p = jnp.exp(s - m_new)
        l = alpha * l + jnp.sum(p, axis=-1, keepdims=True)
        acc = alpha * acc + _dot(p.astype(BF16), v)
        return m_new, l, acc

    ck = PAGES_PER_CHUNK
    for c in range(n_pages // ck):
        kc = ckv_buf[slot, ck * c:ck * (c + 1)].reshape(ck * PAGE_SIZE, KV_LORA).astype(BF16)
        kr_t = jnp.concatenate([kr_buf[slot, ck * c + pg] for pg in range(ck)], axis=1).astype(BF16)
        s = (_dot_nt(qa, kc) + _dot(qp, kr_t)) * ATTN_SCALE
        m, l, acc = online(s, kc, m, l, acc)

    kn = kn_ref[...]
    s = _dot_nt(q, kn) * ATTN_SCALE
    steps = rows // MLA_HEADS
    q_step = lax.broadcasted_iota(jnp.int32, (MLA_HEADS, steps, kn.shape[0]), 1).reshape(rows, kn.shape[0])
    k_step = lax.broadcasted_iota(jnp.int32, (rows, kn.shape[0]), 1)
    s = jnp.where(k_step <= q_step, s, NEG_INF)
    m, l, acc = online(s, kn[:, 0:KV_LORA], m, l, acc)
    o_ref[...] = acc / l


def attn_sample(page_table, q, kn, cache_ckv, cache_kr_t):
    nb, rows, _ = q.shape
    n_pages = page_table.shape[1]
    grid_spec = pltpu.PrefetchScalarGridSpec(
        num_scalar_prefetch=1,
        grid=(nb,),
        in_specs=[pl.BlockSpec((None, rows, QK_PAD), lambda b, pt: (b, 0, 0)),
                  pl.BlockSpec((None, kn.shape[1], QK_PAD), lambda b, pt: (b, 0, 0)),
                  pl.BlockSpec(memory_space=pl.ANY),
                  pl.BlockSpec(memory_space=pl.ANY)],
        out_specs=pl.BlockSpec((None, rows, KV_LORA), lambda b, pt: (b, 0, 0)),
        scratch_shapes=[pltpu.VMEM((2, n_pages, PAGE_SIZE, KV_LORA), F32),
                        pltpu.VMEM((2, n_pages, ROPE_DIM, PAGE_SIZE), F32),
                        pltpu.SemaphoreType.DMA((2, 2))],
    )
    return pl.pallas_call(
        _attn_sample_kernel,
        grid_spec=grid_spec,
        out_shape=jax.ShapeDtypeStruct((nb, rows, KV_LORA), F32),
        compiler_params=_params(1),
        name="attn_sample",
    )(page_table, q, kn, cache_ckv, cache_kr_t)


def _vproj_kernel(o_ref, wv_ref, out_ref):
    out_ref[...] = _dot(o_ref[...], wv_ref[...]).astype(BF16)


def vproj(o_lat, wv):
    heads, t, _ = o_lat.shape
    return pl.pallas_call(
        _vproj_kernel,
        grid=(heads,),
        in_specs=[pl.BlockSpec((None, t, KV_LORA), lambda h: (h, 0, 0)),
                  pl.BlockSpec((None, KV_LORA, V_DIM), lambda h: (h, 0, 0))],
        out_specs=pl.BlockSpec((t, V_DIM), lambda h: (0, h)),
        out_shape=jax.ShapeDtypeStruct((t, heads * V_DIM), BF16),
        compiler_params=_params(1),
        name="vproj",
    )(o_lat, wv)


def _mem_attn_kernel(q_ref, k_ref, v_ref, o_ref):
    q = q_ref[...].astype(BF16)
    for h in range(MEM_HEADS):
        sl = slice(MEM_HEAD_DIM * h, MEM_HEAD_DIM * (h + 1))
        s = _dot_nt(q[:, sl], k_ref[:, h, :].astype(BF16)) * MEM_SCALE
        e = jnp.exp(s - jnp.max(s, axis=-1, keepdims=True))
        p = e / jnp.sum(e, axis=-1, keepdims=True)
        o_ref[:, sl] = _dot(p.astype(BF16), v_ref[:, h, :].astype(BF16)).astype(BF16)


def mem_attn(q, q_col_block, mem_k, mem_v, rows_per_batch, tm):
    batch = mem_k.shape[0]
    width = MEM_HEADS * MEM_HEAD_DIM
    nc = rows_per_batch // tm
    kv_spec = pl.BlockSpec((None, MEM_TOKENS, MEM_HEADS, MEM_HEAD_DIM), lambda b, c: (b, 0, 0, 0))
    return pl.pallas_call(
        _mem_attn_kernel,
        grid=(batch, nc),
        in_specs=[pl.BlockSpec((tm, width), lambda b, c: (b * nc + c, q_col_block)), kv_spec, kv_spec],
        out_specs=pl.BlockSpec((tm, width), lambda b, c: (b * nc + c, 0)),
        out_shape=jax.ShapeDtypeStruct((batch * rows_per_batch, width), BF16),
        compiler_params=_params(2),
        name="mem_attn",
    )(q, mem_k, mem_v)


def _merge_kernel(x_ref, gl_ref, ga_ref, gm_ref, yl_ref, ya_ref, ym_ref, wl_ref, wa_ref, wm_ref, wo_ref,
                  o_ref, acc):
    j = pl.program_id(1)

    @pl.when(j == 0)
    def _():
        acc[...] = jnp.zeros_like(acc)

    merged = (jax.nn.sigmoid(gl_ref[...]) * _dot(yl_ref[...], wl_ref[...])
              + jax.nn.sigmoid(ga_ref[...]) * _dot(ya_ref[...], wa_ref[...])
              + jax.nn.sigmoid(gm_ref[...]) * _dot(ym_ref[...], wm_ref[...]))
    acc[...] += _dot(merged.astype(BF16), wo_ref[...])

    @pl.when(j == pl.num_programs(1) - 1)
    def _():
        o_ref[...] = x_ref[...] + acc[...]


def merge(x, p, y_lru, o_mla, o_mem, ow, tm, tn=512):
    t = x.shape[0]
    nj = D_MODEL // tn
    gate = lambda k: pl.BlockSpec((tm, tn), lambda i, j: (i, (OFF_GATES + k * D_MODEL) // tn + j))
    act = lambda a: pl.BlockSpec((tm, a.shape[1]), lambda i, j: (i, 0))
    wcol = lambda w: pl.BlockSpec((w.shape[0], tn), lambda i, j: (0, j))
    return pl.pallas_call(
        _merge_kernel,
        grid=(t // tm, nj),
        in_specs=[pl.BlockSpec((tm, D_MODEL), lambda i, j: (i, 0)), gate(0), gate(1), gate(2),
                  act(y_lru), act(o_mla), act(o_mem),
                  wcol(ow["wl"]), wcol(ow["wa"]), wcol(ow["wm"]),
                  pl.BlockSpec((tn, D_MODEL), lambda i, j: (j, 0))],
        out_specs=pl.BlockSpec((tm, D_MODEL), lambda i, j: (i, 0)),
        out_shape=jax.ShapeDtypeStruct((t, D_MODEL), F32),
        scratch_shapes=[pltpu.VMEM((tm, D_MODEL), F32)],
        compiler_params=_params(2),
        name="merge",
    )(x, p, p, p, y_lru, o_mla, o_mem, ow["wl"], ow["wa"], ow["wm"], ow["wo"])


N_TOP = PEER_TOPK + 1
CAND_PAIRS = [(i, j) for i in range(N_TOP) for j in range(N_TOP) if (i + 1) * (j + 1) <= N_TOP]
CAND_ROWS = -(-len(CAND_PAIRS) // 8) * 8
TOP_ROWS = -(-N_TOP // 8) * 8


def _top_values(s, n, out_ref):
    for r in range(n):
        m = jnp.max(s, axis=0, keepdims=True)
        out_ref[r:r + 1, :] = m
        s = jnp.where(s == m, NEG_INF, s)


def _peer_route_kernel(x_ref, g_ref, wq_ref, k_ref, ht_ref, thr_ref, cf_ref, s2_ref, p2_ref,
                       hb, top_a, top_b, cand, top_c):
    @pl.when(pl.program_id(1) == 0)
    def _():
        hf = _rms(x_ref[...], g_ref[...])
        hb[...] = hf.astype(BF16)
        ht_ref[...] = hf.T.astype(BF16)

    qb = _dot(hb[...], wq_ref[...]).astype(BF16)
    s1 = _dot_nt(k_ref[0], qb[:, 0:PEER_KEYS])
    s2 = _dot_nt(k_ref[1], qb[:, PEER_KEYS:2 * PEER_KEYS])
    _top_values(s1, N_TOP, top_a)
    _top_values(s2, N_TOP, top_b)
    cand[...] = jnp.full(cand.shape, NEG_INF, F32)
    for n, (i, j) in enumerate(CAND_PAIRS):
        cand[n:n + 1, :] = top_a[i:i + 1, :] + top_b[j:j + 1, :]
    _top_values(cand[...], N_TOP, top_c)
    c = top_c[...]
    c_max = c[0:1, :]
    z = jnp.sum(jnp.exp(c[0:PEER_TOPK, :] - c_max), axis=0, keepdims=True)
    tau = 0.5 * (c[PEER_TOPK - 1:PEER_TOPK, :] + c[PEER_TOPK:PEER_TOPK + 1, :])
    thr_ref[...] = tau - s1
    cf_ref[...] = jnp.exp(s1 - top_a[0:1, :]) / z
    s2_ref[...] = s2
    p2_ref[...] = jnp.exp(s2 - top_b[0:1, :])


def peer_route(x, g, wq, keys, tm):
    t = x.shape[0]
    stat = pl.BlockSpec((None, PEER_KEYS, tm), lambda i, h: (h, 0, i))
    stat_shape = jax.ShapeDtypeStruct((PEER_HEADS, PEER_KEYS, t), F32)
    return pl.pallas_call(
        _peer_route_kernel,
        grid=(t // tm, PEER_HEADS),
        in_specs=[pl.BlockSpec((tm, D_MODEL), lambda i, h: (i, 0)),
                  pl.BlockSpec((1, D_MODEL), lambda i, h: (0, 0)),
                  pl.BlockSpec((D_MODEL, 2 * PEER_KEYS), lambda i, h: (0, h)),
                  pl.BlockSpec((None, 2, PEER_KEYS, PEER_KEYS), lambda i, h: (h, 0, 0, 0))],
        out_specs=[pl.BlockSpec((D_MODEL, tm), lambda i, h: (0, i)), stat, stat, stat, stat],
        out_shape=[jax.ShapeDtypeStruct((D_MODEL, t), BF16), stat_shape, stat_shape, stat_shape, stat_shape],
        scratch_shapes=[pltpu.VMEM((tm, D_MODEL), BF16), pltpu.VMEM((TOP_ROWS, tm), F32),
                        pltpu.VMEM((TOP_ROWS, tm), F32), pltpu.VMEM((CAND_ROWS, tm), F32),
                        pltpu.VMEM((TOP_ROWS, tm), F32)],
        compiler_params=_params(2),
        name="peer_route",
    )(x, g.reshape(1, D_MODEL), wq, keys)


PEER_CHUNK = 8 * PEER_KEYS
PEER_SUB = 2 * PEER_KEYS


def _peer_mix_kernel(ht_ref, u_ref, v_ref, thr_ref, cf_ref, s2_ref, p2_ref, x_ref, g_ref, y_ref, coef_scr):
    j = pl.program_id(1)
    ec = u_ref.shape[0]
    tm = ht_ref.shape[1]

    @pl.when(j == 0)
    def _():
        y_ref[...] = x_ref[...]

    for sb in range(ec // PEER_SUB):
        act = _dot(u_ref[PEER_SUB * sb:PEER_SUB * (sb + 1), :], ht_ref[...])
        for e in range(PEER_SUB // PEER_KEYS):
            el = sb * (PEER_SUB // PEER_KEYS) + e
            rows = slice(PEER_KEYS * e, PEER_KEYS * (e + 1))
            for lt in range(tm // LANES):
                lanes = slice(LANES * lt, LANES * (lt + 1))
                c = None
                for h in range(PEER_HEADS):
                    term = jnp.where(s2_ref[h, :, lanes] > thr_ref[h, el:el + 1, lanes],
                                     p2_ref[h, :, lanes] * cf_ref[h, el:el + 1, lanes], 0.0)
                    c = term if c is None else c + term
                coef_scr[PEER_KEYS * el:PEER_KEYS * (el + 1), lanes] = (c * _gelu(act[rows, lanes])).astype(BF16)

    y_ref[...] += _dot_tn(coef_scr[...], v_ref[...])

    @pl.when(j == pl.num_programs(1) - 1)
    def _():
        y_ref[...] = _rms(y_ref[...], g_ref[...])


def peer_mix(ht, u, v, thr, cf, s2, p2, x, g, tm):
    t = x.shape[0]
    ec = PEER_CHUNK
    rows = pl.BlockSpec((PEER_HEADS, ec // PEER_KEYS, tm), lambda i, j: (0, j, i))
    full = pl.BlockSpec((PEER_HEADS, PEER_KEYS, tm), lambda i, j: (0, 0, i))
    return pl.pallas_call(
        _peer_mix_kernel,
        grid=(t // tm, PEER_EXPERTS // ec),
        in_specs=[pl.BlockSpec((D_MODEL, tm), lambda i, j: (0, i)),
                  pl.BlockSpec((ec, D_MODEL), lambda i, j: (j, 0)),
                  pl.BlockSpec((ec, D_MODEL), lambda i, j: (j, 0)),
                  rows, rows, full, full,
                  pl.BlockSpec((tm, D_MODEL), lambda i, j: (i, 0)),
                  pl.BlockSpec((1, D_MODEL), lambda i, j: (0, 0))],
        out_specs=pl.BlockSpec((tm, D_MODEL), lambda i, j: (i, 0)),
        out_shape=jax.ShapeDtypeStruct((t, D_MODEL), F32),
        scratch_shapes=[pltpu.VMEM((ec, tm), BF16)],
        compiler_params=_params(2),
        name="peer_mix",
    )(ht, u, v, thr, cf, s2, p2, x, g.reshape(1, D_MODEL))


def _block_diag(w, per_tile):
    nb, bw, _ = w.shape
    w4 = w.reshape(nb // per_tile, per_tile, bw, bw)
    eye = jnp.eye(per_tile, dtype=w.dtype)
    tiles = w4[:, :, :, None, :] * eye[None, :, None, :, None]
    return tiles.reshape(nb // per_tile, per_tile * bw, per_tile * bw).astype(BF16)


def _rope_tables(pos):
    half = ROPE_DIM // 2
    inv = ROPE_THETA ** (-jnp.arange(half, dtype=F32) / half)
    ang = pos.astype(F32)[:, None] * inv[None, :]
    cos, sin = jnp.cos(ang), jnp.sin(ang)
    pad = jnp.zeros((pos.shape[0], LANES - ROPE_DIM), F32)
    return (jnp.concatenate([cos, cos, pad], axis=1), jnp.concatenate([-sin, sin, pad], axis=1))


def _prepare_weights(w_in, conv_w, conv_b, w_rec_a, b_rec_a, w_rec_x, b_rec_x, lru_lambda, w_o_lru,
                     g_q_a, w_q_b, g_kv_a, w_kv_b, w_o_mla, w_mem_k, w_mem_v, w_o_mem, w_out,
                     w_peer_query, peer_sub_keys, peer_u, peer_v):
    offs = np.cumsum([D_RNN, D_RNN, Q_LORA, KV_LORA + ROPE_DIM, MEM_HEADS * MEM_HEAD_DIM]).tolist()
    w_xr, w_gr, w_qa, w_kva, w_qm, w_gates = jnp.split(w_in, offs, axis=1)
    pad = jnp.zeros((D_MODEL, P_WIDTH - OFF_KR - ROPE_DIM), w_in.dtype)
    w_p = jnp.concatenate([w_gates, w_xr, w_gr, w_qm, w_qa, w_kva, pad], axis=1).astype(BF16)
    row = lambda v: v.reshape(1, -1)
    lw = dict(cw=conv_w, cb=row(conv_b), wa=_block_diag(w_rec_a, 4), ba=row(b_rec_a),
              wx=_block_diag(w_rec_x, 4), bx=row(b_rec_x), lam=row(lru_lambda))
    w_pe = jnp.pad(w_q_b[:, :, NOPE_DIM:], ((0, 0), (0, 0), (0, LANES - ROPE_DIM)))
    mw = dict(gq=row(g_q_a), gkv=row(g_kv_a),
              wn=w_q_b[:, :, :NOPE_DIM].reshape(Q_LORA, MLA_HEADS * NOPE_DIM).astype(BF16),
              wp=w_pe.reshape(Q_LORA, MLA_HEADS * LANES).astype(BF16),
              wkn=jnp.transpose(w_kv_b[:, :, :NOPE_DIM], (1, 2, 0)).astype(BF16),
              wv=jnp.transpose(w_kv_b[:, :, NOPE_DIM:], (1, 0, 2)).astype(BF16))
    ow = dict(wl=w_o_lru.astype(BF16), wa=w_o_mla.astype(BF16), wm=w_o_mem.astype(BF16), wo=w_out.astype(BF16))
    w_mem = jnp.concatenate([w_mem_k.reshape(D_MODEL, -1), w_mem_v.reshape(D_MODEL, -1)], axis=1).astype(BF16)
    pw = dict(wq=w_peer_query.reshape(D_MODEL, -1).astype(BF16), keys=peer_sub_keys.astype(BF16),
              u=peer_u.astype(BF16), v=peer_v.astype(BF16))
    return w_p, lw, mw, ow, w_mem, pw


def _peer_and_norm(x2, g_ffn, g_final, pw, tm):
    ht, thr, cf, s2, p2 = peer_route(x2, g_ffn, pw["wq"], pw["keys"], tm)
    return peer_mix(ht, pw["u"], pw["v"], thr, cf, s2, p2, x2, g_final, tm)


def kernel(x_prompt, x_sample, cache_mla_ckv, cache_mla_krope, cache_mem_k, cache_mem_v, state_lru_h, state_lru_conv, page_table, mem_prompt, g_mix, w_in, conv_w, conv_b, w_rec_a, b_rec_a, w_rec_x, b_rec_x, lru_lambda, w_o_lru, g_q_a, w_q_b, g_kv_a, w_kv_b, w_o_mla, g_mem, w_mem_k, w_mem_v, w_o_mem, w_out, g_ffn, w_peer_query, peer_sub_keys, peer_u, peer_v, g_final):
    batch, seq, _ = x_prompt.shape
    dec_batch, dec_seq, _ = x_sample.shape
    past_len = page_table.shape[1] * PAGE_SIZE
    tp, ts = batch * seq, dec_batch * dec_seq
    tm = 512
    mem_width = MEM_HEADS * MEM_HEAD_DIM

    w_p, lw, mw, ow, w_mem, pw = _prepare_weights(
        w_in, conv_w, conv_b, w_rec_a, b_rec_a, w_rec_x, b_rec_x, lru_lambda, w_o_lru, g_q_a, w_q_b, g_kv_a,
        w_kv_b, w_o_mla, w_mem_k, w_mem_v, w_o_mem, w_out, w_peer_query, peer_sub_keys, peer_u, peer_v)

    xp = x_prompt.reshape(tp, D_MODEL)
    head_shape = (MEM_TOKENS, MEM_HEADS, MEM_HEAD_DIM)
    pp = norm_matmul(xp, g_mix, w_p, tm, PROJ_TN, "in_proj")
    mem_kv = norm_matmul(mem_prompt.reshape(batch * MEM_TOKENS, D_MODEL), g_mem, w_mem, tm, PROJ_TN, "mem_kv")
    mem_k_p = mem_kv[:, :mem_width].reshape((batch,) + head_shape)
    mem_v_p = mem_kv[:, mem_width:].reshape((batch,) + head_shape)
    y_lru_p, lru_h_p = lru_prompt(pp, batch, seq, lw)
    cos_p, sin_p = _rope_tables(jnp.tile(jnp.arange(seq), batch))
    q_p, kcat_p, ckv_p, kr_p = mla_prep(pp, cos_p, sin_p, mw, tm)
    o_mla_p = attn_prompt(q_p, kcat_p, mw["wv"], batch, seq)
    o_mem_p = mem_attn(pp, OFF_QM // mem_width, mem_k_p, mem_v_p, seq, tm)
    x2_p = merge(xp, pp, y_lru_p, o_mla_p, o_mem_p, ow, tm)
    y_p = _peer_and_norm(x2_p, g_ffn, g_final, pw, tm)

    xs = x_sample.reshape(ts, D_MODEL)
    ps = norm_matmul(xs, g_mix, w_p, tm, PROJ_TN, "in_proj")
    step_major = lambda a: jnp.transpose(a.reshape(dec_batch, dec_seq, -1), (1, 0, 2))
    xr_s = ps[:, OFF_XR:OFF_XR + D_RNN]
    y_lru_s, lru_h_s = lru_sample(step_major(xr_s), step_major(ps[:, OFF_GR:OFF_GR + D_RNN]),
                                  jnp.transpose(state_lru_conv, (1, 0, 2)), state_lru_h, lw)
    y_lru_s = jnp.transpose(y_lru_s, (1, 0, 2)).reshape(ts, D_RNN)
    cos_s, sin_s = _rope_tables(jnp.tile(past_len + jnp.arange(dec_seq), dec_batch))
    q_s, kcat_s, ckv_s, kr_s = mla_prep(ps, cos_s, sin_s, mw, tm)
    q_rows = jnp.transpose(q_s.reshape(MLA_HEADS, dec_batch, dec_seq, QK_PAD), (1, 0, 2, 3))
    q_rows = q_rows.reshape(dec_batch, MLA_HEADS * dec_seq, QK_PAD)
    k_new = jnp.pad(kcat_s.reshape(dec_batch, dec_seq, QK_PAD), ((0, 0), (0, PAGE_SIZE - dec_seq), (0, 0)))
    o_lat = attn_sample(page_table, q_rows, k_new, cache_mla_ckv, jnp.swapaxes(cache_mla_krope, 1, 2))
    o_lat = jnp.transpose(o_lat.reshape(dec_batch, MLA_HEADS, dec_seq, KV_LORA), (1, 0, 2, 3))
    o_mla_s = vproj(o_lat.reshape(MLA_HEADS, ts, KV_LORA).astype(BF16), mw["wv"])
    q_pad_rows = 16
    qm_s = jnp.pad(ps[:, OFF_QM:OFF_QM + mem_width].reshape(dec_batch, dec_seq, mem_width),
                   ((0, 0), (0, q_pad_rows - dec_seq), (0, 0))).reshape(dec_batch * q_pad_rows, mem_width)
    o_mem_s = mem_attn(qm_s, 0, cache_mem_k, cache_mem_v, q_pad_rows, q_pad_rows)
    o_mem_s = o_mem_s.reshape(dec_batch, q_pad_rows, mem_width)[:, :dec_seq].reshape(ts, mem_width)
    x2_s = merge(xs, ps, y_lru_s, o_mla_s, o_mem_s, ow, tm)
    y_s = _peer_and_norm(x2_s, g_ffn, g_final, pw, tm)

    conv_tail = CONV_W - 1
    return (y_p.reshape(batch, seq, D_MODEL), y_s.reshape(dec_batch, dec_seq, D_MODEL),
            ckv_p.reshape(batch, seq, KV_LORA), kr_p[:, :ROPE_DIM].reshape(batch, seq, ROPE_DIM),
            mem_k_p, mem_v_p,
            lru_h_p, pp[:, OFF_XR:OFF_XR + D_RNN].reshape(batch, seq, D_RNN)[:, seq - conv_tail:],
            ckv_s.reshape(dec_batch, dec_seq, KV_LORA), kr_s[:, :ROPE_DIM].reshape(dec_batch, dec_seq, ROPE_DIM),
            lru_h_s, xr_s.reshape(dec_batch, dec_seq, D_RNN)[:, dec_seq - conv_tail:])
```

```python
import functools
import math

import jax
import jax.numpy as jnp
import numpy as np
from jax import lax
from jax.experimental import pallas as pl
from jax.experimental.pallas import tpu as pltpu

F32 = jnp.float32
BF16 = jnp.bfloat16

D_MODEL = 2048
EPS = 1e-6
D_RNN = 1024
LRU_BLOCKS = 16
CONV_W = 4
LRU_C = 8.0
MLA_HEADS = 16
Q_LORA = 512
KV_LORA = 256
NOPE_DIM = 128
ROPE_DIM = 64
V_DIM = 128
ROPE_THETA = 10000.0
MEM_TOKENS = 256
MEM_HEADS = 4
MEM_HEAD_DIM = 256
PEER_HEADS = 8
PEER_KEYS = 128
PEER_EXPERTS = PEER_KEYS * PEER_KEYS
PEER_TOPK = 16
PAGE_SIZE = 128

LANES = 128
QK_PAD = KV_LORA + LANES
ATTN_SCALE = (NOPE_DIM + ROPE_DIM) ** -0.5
MEM_SCALE = MEM_HEAD_DIM ** -0.5
NEG_INF = float("-inf")

OFF_GATES = 0
OFF_XR = 3 * D_MODEL
OFF_GR = OFF_XR + D_RNN
OFF_QM = OFF_GR + D_RNN
OFF_QA = OFF_QM + MEM_HEADS * MEM_HEAD_DIM
OFF_CKV = OFF_QA + Q_LORA
OFF_KR = OFF_CKV + KV_LORA
P_WIDTH = OFF_KR + 2 * LANES
PROJ_TN = 2048

VMEM_LIMIT = 56 * 1024 * 1024


def _params(n_grid):
    return pltpu.CompilerParams(dimension_semantics=("arbitrary",) * n_grid, vmem_limit_bytes=VMEM_LIMIT)


def _rms(x, g):
    ms = jnp.mean(x * x, axis=-1, keepdims=True)
    return x * lax.rsqrt(ms + EPS) * g


def _gelu(x):
    cdf = 0.5 * (1.0 + jnp.tanh(math.sqrt(2.0 / math.pi) * (x + 0.044715 * (x * x * x))))
    return x * cdf


def _dot(a, b):
    return jnp.dot(a, b, preferred_element_type=F32)


def _dot_nt(a, b):
    return lax.dot_general(a, b, (((1,), (1,)), ((), ())), preferred_element_type=F32)


def _dot_tn(a, b):
    return lax.dot_general(a, b, (((0,), (0,)), ((), ())), preferred_element_type=F32)


def _norm_matmul_kernel(x_ref, g_ref, w_ref, o_ref, h_scr):
    @pl.when(pl.program_id(1) == 0)
    def _():
        h_scr[...] = _rms(x_ref[...], g_ref[...]).astype(BF16)

    o_ref[...] = _dot(h_scr[...], w_ref[...])


def norm_matmul(x, g, w, tm, tn, name):
    t, d = x.shape
    n = w.shape[1]
    return pl.pallas_call(
        _norm_matmul_kernel,
        grid=(t // tm, n // tn),
        in_specs=[pl.BlockSpec((tm, d), lambda i, j: (i, 0)),
                  pl.BlockSpec((1, d), lambda i, j: (0, 0)),
                  pl.BlockSpec((d, tn), lambda i, j: (0, j))],
        out_specs=pl.BlockSpec((tm, tn), lambda i, j: (i, j)),
        out_shape=jax.ShapeDtypeStruct((t, n), F32),
        scratch_shapes=[pltpu.VMEM((tm, d), BF16)],
        compiler_params=_params(2),
        name=name,
    )(x, g.reshape(1, d), w)


def _lru_gates(xc, wa_ref, ba, wx_ref, bx, lam):
    xb = xc.astype(BF16)
    nk = wa_ref.shape[0]
    wk = D_RNN // nk
    ra = [_dot(xb[:, wk * k:wk * (k + 1)], wa_ref[k]) for k in range(nk)]
    ix = [_dot(xb[:, wk * k:wk * (k + 1)], wx_ref[k]) for k in range(nk)]
    r = jax.nn.sigmoid(jnp.concatenate(ra, axis=1) + ba)
    i = jax.nn.sigmoid(jnp.concatenate(ix, axis=1) + bx)
    neg_lam = -lam
    softplus = jnp.maximum(neg_lam, 0.0) + jnp.log1p(jnp.exp(-jnp.abs(neg_lam)))
    log_a = -LRU_C * r * softplus
    a = jnp.exp(log_a)
    u = jnp.sqrt(-jnp.tanh(log_a) * (a * a + 1.0)) * (i * xc)
    return a, u


def _lru_prompt_kernel(xr_ref, gr_ref, cw_ref, cb_ref, wa_ref, ba_ref, wx_ref, bx_ref, lam_ref,
                       y_ref, hl_ref, xpad, a_scr, u_scr, hs_scr, h_scr):
    tm = xr_ref.shape[0]

    @pl.when(pl.program_id(1) == 0)
    def _():
        xpad[0:8, :] = jnp.zeros((8, D_RNN), F32)
        h_scr[...] = jnp.zeros_like(h_scr)

    xr = xr_ref[...]
    xpad[8:8 + tm, :] = xr
    cw = cw_ref[...]
    xc = cb_ref[...] + (xpad[5:5 + tm, :] * cw[0:1] + xpad[6:6 + tm, :] * cw[1:2]
                        + xpad[7:7 + tm, :] * cw[2:3] + xr * cw[3:4])
    xpad[0:8, :] = xr[tm - 8:tm, :]
    a, u = _lru_gates(xc, wa_ref, ba_ref[...], wx_ref, bx_ref[...], lam_ref[...])
    a_scr[...] = a
    u_scr[...] = u

    def body(g, h):
        r0 = pl.multiple_of(g * 8, 8)
        a8 = a_scr[pl.ds(r0, 8), :]
        u8 = u_scr[pl.ds(r0, 8), :]
        rows = []
        for r in range(8):
            h = a8[r:r + 1, :] * h + u8[r:r + 1, :]
            rows.append(h)
        hs_scr[pl.ds(r0, 8), :] = jnp.concatenate(rows, axis=0)
        return h

    h = lax.fori_loop(0, tm // 8, body, h_scr[0:1, :])
    h_scr[0:1, :] = h
    y_ref[...] = (hs_scr[...] * _gelu(gr_ref[...])).astype(BF16)
    hl_ref[...] = h.reshape(1, 1, D_RNN)


def lru_prompt(p, batch, seq, lw, tm=256):
    nc = seq // tm
    full = lambda shape: pl.BlockSpec(shape, lambda b, c: (0,) * len(shape))
    y, hl = pl.pallas_call(
        _lru_prompt_kernel,
        grid=(batch, nc),
        in_specs=[pl.BlockSpec((tm, D_RNN), lambda b, c: (b * nc + c, OFF_XR // D_RNN)),
                  pl.BlockSpec((tm, D_RNN), lambda b, c: (b * nc + c, OFF_GR // D_RNN)),
                  full((CONV_W, D_RNN)), full((1, D_RNN)),
                  full(lw["wa"].shape), full((1, D_RNN)), full(lw["wx"].shape), full((1, D_RNN)),
                  full((1, D_RNN))],
        out_specs=[pl.BlockSpec((tm, D_RNN), lambda b, c: (b * nc + c, 0)),
                   pl.BlockSpec((1, 1, D_RNN), lambda b, c: (b, 0, 0))],
        out_shape=[jax.ShapeDtypeStruct((batch * seq, D_RNN), BF16),
                   jax.ShapeDtypeStruct((batch, 1, D_RNN), F32)],
        scratch_shapes=[pltpu.VMEM((tm + 8, D_RNN), F32), pltpu.VMEM((tm, D_RNN), F32),
                        pltpu.VMEM((tm, D_RNN), F32), pltpu.VMEM((tm, D_RNN), F32),
                        pltpu.VMEM((8, D_RNN), F32)],
        compiler_params=_params(2),
        name="lru_prompt",
    )(p, p, lw["cw"], lw["cb"], lw["wa"], lw["ba"], lw["wx"], lw["bx"], lw["lam"])
    return y, hl.reshape(batch, D_RNN)


def _lru_sample_kernel(xs_ref, gs_ref, c0_ref, h0_ref, cw_ref, cb_ref, wa_ref, ba_ref, wx_ref, bx_ref, lam_ref,
                       y_ref, hl_ref):
    steps = xs_ref.shape[0]
    cw = cw_ref[...]
    xcat = [c0_ref[k] for k in range(CONV_W - 1)] + [xs_ref[s] for s in range(steps)]
    h = h0_ref[...]
    for s in range(steps):
        xc = cb_ref[...] + (xcat[s] * cw[0:1] + xcat[s + 1] * cw[1:2] + xcat[s + 2] * cw[2:3]
                            + xcat[s + 3] * cw[3:4])
        a, u = _lru_gates(xc, wa_ref, ba_ref[...], wx_ref, bx_ref[...], lam_ref[...])
        h = a * h + u
        y_ref[s] = (h * _gelu(gs_ref[s])).astype(BF16)
    hl_ref[...] = h


def lru_sample(xs, gs, c0, h0, lw):
    steps, batch, _ = xs.shape
    return pl.pallas_call(
        _lru_sample_kernel,
        out_shape=[jax.ShapeDtypeStruct((steps, batch, D_RNN), BF16),
                   jax.ShapeDtypeStruct((batch, D_RNN), F32)],
        compiler_params=pltpu.CompilerParams(vmem_limit_bytes=VMEM_LIMIT),
        name="lru_sample",
    )(xs, gs, c0, h0, lw["cw"], lw["cb"], lw["wa"], lw["ba"], lw["wx"], lw["bx"], lw["lam"])


def _rope(x, cos, sin_signed):
    lane = lax.broadcasted_iota(jnp.int32, x.shape, 1)
    half = ROPE_DIM // 2
    rot = jnp.where(lane < half, pltpu.roll(x, LANES - half, 1), pltpu.roll(x, half, 1))
    return x * cos + rot * sin_signed


def _mla_prep_kernel(qa_ref, ckv_ref, kr_ref, cos_ref, sin_ref, gq_ref, gkv_ref, wn_ref, wp_ref, wkn_ref,
                     q_ref, kcat_ref, ckv_out, kr_out):
    cos = cos_ref[...]
    sin = sin_ref[...]
    qn = _rms(qa_ref[...], gq_ref[...]).astype(BF16)
    q_nope = _dot(qn, wn_ref[...]).astype(BF16)
    q_pe = _dot(qn, wp_ref[...])
    for h in range(MLA_HEADS):
        q_abs = _dot(q_nope[:, NOPE_DIM * h:NOPE_DIM * (h + 1)], wkn_ref[h])
        q_ref[h, :, 0:KV_LORA] = q_abs.astype(BF16)
        q_ref[h, :, KV_LORA:QK_PAD] = _rope(q_pe[:, LANES * h:LANES * (h + 1)], cos, sin).astype(BF16)
    ckv = _rms(ckv_ref[...], gkv_ref[...])
    kr = _rope(kr_ref[...], cos, sin)
    ckv_out[...] = ckv
    kr_out[...] = kr
    kcat_ref[:, 0:KV_LORA] = ckv.astype(BF16)
    kcat_ref[:, KV_LORA:QK_PAD] = kr.astype(BF16)


def mla_prep(p, cos, sin, mw, tm):
    t = p.shape[0]
    full = lambda shape: pl.BlockSpec(shape, lambda i: (0,) * len(shape))
    return pl.pallas_call(
        _mla_prep_kernel,
        grid=(t // tm,),
        in_specs=[pl.BlockSpec((tm, Q_LORA), lambda i: (i, OFF_QA // Q_LORA)),
                  pl.BlockSpec((tm, KV_LORA), lambda i: (i, OFF_CKV // KV_LORA)),
                  pl.BlockSpec((tm, LANES), lambda i: (i, OFF_KR // LANES)),
                  pl.BlockSpec((tm, LANES), lambda i: (i, 0)),
                  pl.BlockSpec((tm, LANES), lambda i: (i, 0)),
                  full((1, Q_LORA)), full((1, KV_LORA)),
                  full(mw["wn"].shape), full(mw["wp"].shape), full(mw["wkn"].shape)],
        out_specs=[pl.BlockSpec((MLA_HEADS, tm, QK_PAD), lambda i: (0, i, 0)),
                   pl.BlockSpec((tm, QK_PAD), lambda i: (i, 0)),
                   pl.BlockSpec((tm, KV_LORA), lambda i: (i, 0)),
                   pl.BlockSpec((tm, LANES), lambda i: (i, 0))],
        out_shape=[jax.ShapeDtypeStruct((MLA_HEADS, t, QK_PAD), BF16),
                   jax.ShapeDtypeStruct((t, QK_PAD), BF16),
                   jax.ShapeDtypeStruct((t, KV_LORA), F32),
                   jax.ShapeDtypeStruct((t, LANES), F32)],
        compiler_params=_params(1),
        name="mla_prep",
    )(p, p, p, cos, sin, mw["gq"], mw["gkv"], mw["wn"], mw["wp"], mw["wkn"])


ATTN_HEAD_GROUP = 4

def _attn_prompt_kernel(q_ref, k_ref, wv_ref, o_ref, m_scr, l_scr, acc_scr):
    qi = pl.program_id(1)
    ki = pl.program_id(2)
    heads, tq, _ = q_ref.shape
    tk = k_ref.shape[0]

    @pl.when(ki == 0)
    def _():
        m_scr[...] = jnp.full_like(m_scr, NEG_INF)
        l_scr[...] = jnp.zeros_like(l_scr)
        acc_scr[...] = jnp.zeros_like(acc_scr)

    def lanes(x, width):
        return jnp.concatenate([x] * (width // LANES), axis=1)

    def block(masked):
        k = k_ref[...]
        v = k[:, 0:KV_LORA]
        g = ATTN_HEAD_GROUP
        if masked:
            col = lax.broadcasted_iota(jnp.int32, (g, tq, tk), 2).reshape(g * tq, tk)
            row = lax.broadcasted_iota(jnp.int32, (g, tq, tk), 1).reshape(g * tq, tk)
            visible = col <= row
        for hg in range(heads // g):
            rs = slice(hg * g * tq, (hg + 1) * g * tq)
            s = _dot_nt(q_ref[hg * g:(hg + 1) * g].reshape(g * tq, QK_PAD), k)
            if masked:
                s = jnp.where(visible, s, NEG_INF)
            m_old = m_scr[rs, :]
            m_new = jnp.maximum(m_old, jnp.max(s, axis=-1, keepdims=True))
            alpha = jnp.exp((m_old - m_new) * ATTN_SCALE)
            p = jnp.exp((s - lanes(m_new, tk)) * ATTN_SCALE)
            l_scr[rs, :] = alpha * l_scr[rs, :] + jnp.sum(p, axis=-1, keepdims=True)
            acc_scr[rs, :] = lanes(alpha, KV_LORA) * acc_scr[rs, :] + _dot(p.astype(BF16), v)
            m_scr[rs, :] = m_new

    @pl.when(ki < qi)
    def _():
        block(False)

    @pl.when(ki == qi)
    def _():
        block(True)
        for h in range(heads):
            rs = slice(h * tq, (h + 1) * tq)
            o = (acc_scr[rs, :] / lanes(l_scr[rs, :], KV_LORA)).astype(BF16)
            o_ref[:, V_DIM * h:V_DIM * (h + 1)] = _dot(o, wv_ref[h]).astype(BF16)


def attn_prompt(q, kcat, wv, batch, seq, tq=256):
    nq = seq // tq
    return pl.pallas_call(
        _attn_prompt_kernel,
        grid=(batch, nq, nq),
        in_specs=[pl.BlockSpec((MLA_HEADS, tq, QK_PAD), lambda b, i, j: (0, b * nq + i, 0)),
                  pl.BlockSpec((tq, QK_PAD), lambda b, i, j: (b * nq + jnp.minimum(i, j), 0)),
                  pl.BlockSpec(wv.shape, lambda b, i, j: (0, 0, 0))],
        out_specs=pl.BlockSpec((tq, MLA_HEADS * V_DIM), lambda b, i, j: (b * nq + i, 0)),
        out_shape=jax.ShapeDtypeStruct((batch * seq, MLA_HEADS * V_DIM), BF16),
        scratch_shapes=[pltpu.VMEM((MLA_HEADS * tq, LANES), F32), pltpu.VMEM((MLA_HEADS * tq, LANES), F32),
                        pltpu.VMEM((MLA_HEADS * tq, KV_LORA), F32)],
        compiler_params=_params(3),
        name="attn_prompt",
    )(q, kcat, wv)


PAGES_PER_CHUNK = 32


def _attn_sample_kernel(pt_ref, q_ref, kn_ref, ckv_hbm, kr_hbm, o_ref, ckv_buf, kr_buf, sem):
    b = pl.program_id(0)
    nb = pl.num_programs(0)
    n_pages = ckv_buf.shape[1]
    slot = b % 2

    def page_copies(bb, sl):
        out = []
        for pg in range(n_pages):
            page = pt_ref[bb, pg]
            out.append(pltpu.make_async_copy(ckv_hbm.at[page], ckv_buf.at[sl, pg], sem.at[0, sl]))
            out.append(pltpu.make_async_copy(kr_hbm.at[page], kr_buf.at[sl, pg], sem.at[1, sl]))
        return out

    @pl.when(b == 0)
    def _():
        for c in page_copies(0, 0):
            c.start()

    @pl.when(b + 1 < nb)
    def _():
        for c in page_copies(b + 1, 1 - slot):
            c.start()

    for c in page_copies(b, slot):
        c.wait()

    q = q_ref[...]
    rows = q.shape[0]
    qa = q[:, 0:KV_LORA]
    qp = q[:, KV_LORA:KV_LORA + ROPE_DIM]
    m = jnp.full((rows, 1), NEG_INF, F32)
    l = jnp.zeros((rows, 1), F32)
    acc = jnp.zeros((rows, KV_LORA), F32)

    def online(s, v, m, l, acc):
        m_new = jnp.maximum(m, jnp.max(s, axis=-1, keepdims=True))
        alpha = jnp.exp(m - m_new)
        p = jnp.exp(s - m_new)
        l = alpha * l + jnp.sum(p, axis=-1, keepdims=True)
        acc = alpha * acc + _dot(p.astype(BF16), v)
        return m_new, l, acc

    ck = PAGES_PER_CHUNK
    for c in range(n_pages // ck):
        kc = ckv_buf[slot, ck * c:ck * (c + 1)].reshape(ck * PAGE_SIZE, KV_LORA).astype(BF16)
        kr_t = jnp.concatenate([kr_buf[slot, ck * c + pg] for pg in range(ck)], axis=1).astype(BF16)
        s = (_dot_nt(qa, kc) + _dot(qp, kr_t)) * ATTN_SCALE
        m, l, acc = online(s, kc, m, l, acc)

    kn = kn_ref[...]
    s = _dot_nt(q, kn) * ATTN_SCALE
    steps = rows // MLA_HEADS
    q_step = lax.broadcasted_iota(jnp.int32, (MLA_HEADS, steps, kn.shape[0]), 1).reshape(rows, kn.shape[0])
    k_step = lax.broadcasted_iota(jnp.int32, (rows, kn.shape[0]), 1)
    s = jnp.where(k_step <= q_step, s, NEG_INF)
    m, l, acc = online(s, kn[:, 0:KV_LORA], m, l, acc)
    o_ref[...] = acc / l


def attn_sample(page_table, q, kn, cache_ckv, cache_kr_t):
    nb, rows, _ = q.shape
    n_pages = page_table.shape[1]
    grid_spec = pltpu.PrefetchScalarGridSpec(
        num_scalar_prefetch=1,
        grid=(nb,),
        in_specs=[pl.BlockSpec((None, rows, QK_PAD), lambda b, pt: (b, 0, 0)),
                  pl.BlockSpec((None, kn.shape[1], QK_PAD), lambda b, pt: (b, 0, 0)),
                  pl.BlockSpec(memory_space=pl.ANY),
                  pl.BlockSpec(memory_space=pl.ANY)],
        out_specs=pl.BlockSpec((None, rows, KV_LORA), lambda b, pt: (b, 0, 0)),
        scratch_shapes=[pltpu.VMEM((2, n_pages, PAGE_SIZE, KV_LORA), F32),
                        pltpu.VMEM((2, n_pages, ROPE_DIM, PAGE_SIZE), F32),
                        pltpu.SemaphoreType.DMA((2, 2))],
    )
    return pl.pallas_call(
        _attn_sample_kernel,
        grid_spec=grid_spec,
        out_shape=jax.ShapeDtypeStruct((nb, rows, KV_LORA), F32),
        compiler_params=_params(1),
        name="attn_sample",
    )(page_table, q, kn, cache_ckv, cache_kr_t)


def _vproj_kernel(o_ref, wv_ref, out_ref):
    out_ref[...] = _dot(o_ref[...], wv_ref[...]).astype(BF16)


def vproj(o_lat, wv):
    heads, t, _ = o_lat.shape
    return pl.pallas_call(
        _vproj_kernel,
        grid=(heads,),
        in_specs=[pl.BlockSpec((None, t, KV_LORA), lambda h: (h, 0, 0)),
                  pl.BlockSpec((None, KV_LORA, V_DIM), lambda h: (h, 0, 0))],
        out_specs=pl.BlockSpec((t, V_DIM), lambda h: (0, h)),
        out_shape=jax.ShapeDtypeStruct((t, heads * V_DIM), BF16),
        compiler_params=_params(1),
        name="vproj",
    )(o_lat, wv)


def _mem_heads(q, head_k, head_v, o_ref):
    q = q.astype(BF16)
    for h in range(MEM_HEADS):
        sl = slice(MEM_HEAD_DIM * h, MEM_HEAD_DIM * (h + 1))
        s = _dot_nt(q[:, sl], head_k(h).astype(BF16)) * MEM_SCALE
        e = jnp.exp(s - jnp.max(s, axis=-1, keepdims=True))
        p = e / jnp.sum(e, axis=-1, keepdims=True)
        o_ref[:, sl] = _dot(p.astype(BF16), head_v(h).astype(BF16)).astype(BF16)


def _mem_attn_kernel(q_ref, k_ref, v_ref, o_ref):
    _mem_heads(q_ref[...], lambda h: k_ref[:, MEM_HEAD_DIM * h:MEM_HEAD_DIM * (h + 1)],
               lambda h: v_ref[:, MEM_HEAD_DIM * h:MEM_HEAD_DIM * (h + 1)], o_ref)


def mem_attn(q, q_col_block, mem_k, mem_v, rows_per_batch, tm):
    batch = mem_k.shape[0]
    width = MEM_HEADS * MEM_HEAD_DIM
    nc = rows_per_batch // tm
    kv_spec = pl.BlockSpec((None, MEM_TOKENS, width), lambda b, c: (b, 0, 0))
    return pl.pallas_call(
        _mem_attn_kernel,
        grid=(batch, nc),
        in_specs=[pl.BlockSpec((tm, width), lambda b, c: (b * nc + c, q_col_block)), kv_spec, kv_spec],
        out_specs=pl.BlockSpec((tm, width), lambda b, c: (b * nc + c, 0)),
        out_shape=jax.ShapeDtypeStruct((batch * rows_per_batch, width), BF16),
        compiler_params=_params(2),
        name="mem_attn",
    )(q, mem_k, mem_v)


def _mem_attn_cache_kernel(q_ref, k_hbm, v_hbm, o_ref, k_buf, v_buf, sem):
    b = pl.program_id(0)
    nb = pl.num_programs(0)
    slot = b % 2

    def head_copies(bb, sl):
        out = []
        for h in range(MEM_HEADS):
            out.append(pltpu.make_async_copy(k_hbm.at[bb, :, h, :], k_buf.at[sl, h], sem.at[0, sl]))
            out.append(pltpu.make_async_copy(v_hbm.at[bb, :, h, :], v_buf.at[sl, h], sem.at[1, sl]))
        return out

    @pl.when(b == 0)
    def _():
        for c in head_copies(0, 0):
            c.start()

    @pl.when(b + 1 < nb)
    def _():
        for c in head_copies(b + 1, 1 - slot):
            c.start()

    for c in head_copies(b, slot):
        c.wait()

    _mem_heads(q_ref[...], lambda h: k_buf[slot, h], lambda h: v_buf[slot, h], o_ref)


def mem_attn_cache(q, cache_k, cache_v):
    batch, rows, width = q.shape
    buf = pltpu.VMEM((2, MEM_HEADS, MEM_TOKENS, MEM_HEAD_DIM), F32)
    return pl.pallas_call(
        _mem_attn_cache_kernel,
        grid=(batch,),
        in_specs=[pl.BlockSpec((None, rows, width), lambda b: (b, 0, 0)),
                  pl.BlockSpec(memory_space=pl.ANY), pl.BlockSpec(memory_space=pl.ANY)],
        out_specs=pl.BlockSpec((None, rows, width), lambda b: (b, 0, 0)),
        out_shape=jax.ShapeDtypeStruct((batch, rows, width), BF16),
        scratch_shapes=[buf, buf, pltpu.SemaphoreType.DMA((2, 2))],
        compiler_params=_params(1),
        name="mem_attn_cache",
    )(q, cache_k, cache_v)


def _merge_kernel(x_ref, gl_ref, ga_ref, gm_ref, yl_ref, ya_ref, ym_ref, wl_ref, wa_ref, wm_ref, wo_ref,
                  o_ref, acc):
    j = pl.program_id(1)

    @pl.when(j == 0)
    def _():
        acc[...] = jnp.zeros_like(acc)

    merged = (jax.nn.sigmoid(gl_ref[...]) * _dot(yl_ref[...], wl_ref[...])
              + jax.nn.sigmoid(ga_ref[...]) * _dot(ya_ref[...], wa_ref[...])
              + jax.nn.sigmoid(gm_ref[...]) * _dot(ym_ref[...], wm_ref[...]))
    acc[...] += _dot(merged.astype(BF16), wo_ref[...])

    @pl.when(j == pl.num_programs(1) - 1)
    def _():
        o_ref[...] = x_ref[...] + acc[...]


def merge(x, p, y_lru, o_mla, o_mem, ow, tm, tn=512):
    t = x.shape[0]
    nj = D_MODEL // tn
    gate = lambda k: pl.BlockSpec((tm, tn), lambda i, j: (i, (OFF_GATES + k * D_MODEL) // tn + j))
    act = lambda a: pl.BlockSpec((tm, a.shape[1]), lambda i, j: (i, 0))
    wcol = lambda w: pl.BlockSpec((w.shape[0], tn), lambda i, j: (0, j))
    return pl.pallas_call(
        _merge_kernel,
        grid=(t // tm, nj),
        in_specs=[pl.BlockSpec((tm, D_MODEL), lambda i, j: (i, 0)), gate(0), gate(1), gate(2),
                  act(y_lru), act(o_mla), act(o_mem),
                  wcol(ow["wl"]), wcol(ow["wa"]), wcol(ow["wm"]),
                  pl.BlockSpec((tn, D_MODEL), lambda i, j: (j, 0))],
        out_specs=pl.BlockSpec((tm, D_MODEL), lambda i, j: (i, 0)),
        out_shape=jax.ShapeDtypeStruct((t, D_MODEL), F32),
        scratch_shapes=[pltpu.VMEM((tm, D_MODEL), F32)],
        compiler_params=_params(2),
        name="merge",
    )(x, p, p, p, y_lru, o_mla, o_mem, ow["wl"], ow["wa"], ow["wm"], ow["wo"])


N_TOP = PEER_TOPK + 1
CAND_PAIRS = [(i, j) for i in range(N_TOP) for j in range(N_TOP) if (i + 1) * (j + 1) <= N_TOP]
CAND_ROWS = -(-len(CAND_PAIRS) // 8) * 8
TOP_ROWS = -(-N_TOP // 8) * 8


def _top_values(s, n, out_ref):
    for r in range(n):
        m = jnp.max(s, axis=0, keepdims=True)
        out_ref[r:r + 1, :] = m
        s = jnp.where(s == m, NEG_INF, s)


def _peer_route_kernel(x_ref, g_ref, wq_ref, k_ref, ht_ref, thr_ref, cf_ref, s2_ref, p2_ref,
                       hb, top_a, top_b, cand, top_c):
    @pl.when(pl.program_id(1) == 0)
    def _():
        hf = _rms(x_ref[...], g_ref[...])
        hb[...] = hf.astype(BF16)
        ht_ref[...] = hf.T.astype(BF16)

    qb = _dot(hb[...], wq_ref[...]).astype(BF16)
    s1 = _dot_nt(k_ref[0], qb[:, 0:PEER_KEYS])
    s2 = _dot_nt(k_ref[1], qb[:, PEER_KEYS:2 * PEER_KEYS])
    _top_values(s1, N_TOP, top_a)
    _top_values(s2, N_TOP, top_b)
    cand[...] = jnp.full(cand.shape, NEG_INF, F32)
    for n, (i, j) in enumerate(CAND_PAIRS):
        cand[n:n + 1, :] = top_a[i:i + 1, :] + top_b[j:j + 1, :]
    _top_values(cand[...], N_TOP, top_c)
    c = top_c[...]
    c_max = c[0:1, :]
    z = jnp.sum(jnp.exp(c[0:PEER_TOPK, :] - c_max), axis=0, keepdims=True)
    tau = 0.5 * (c[PEER_TOPK - 1:PEER_TOPK, :] + c[PEER_TOPK:PEER_TOPK + 1, :])
    thr_ref[...] = tau - s1
    cf_ref[...] = jnp.exp(s1 - top_a[0:1, :]) / z
    s2_ref[...] = s2
    p2_ref[...] = jnp.exp(s2 - top_b[0:1, :])


def peer_route(x, g, wq, keys, tm):
    t = x.shape[0]
    stat = pl.BlockSpec((None, PEER_KEYS, tm), lambda i, h: (h, 0, i))
    stat_shape = jax.ShapeDtypeStruct((PEER_HEADS, PEER_KEYS, t), F32)
    return pl.pallas_call(
        _peer_route_kernel,
        grid=(t // tm, PEER_HEADS),
        in_specs=[pl.BlockSpec((tm, D_MODEL), lambda i, h: (i, 0)),
                  pl.BlockSpec((1, D_MODEL), lambda i, h: (0, 0)),
                  pl.BlockSpec((D_MODEL, 2 * PEER_KEYS), lambda i, h: (0, h)),
                  pl.BlockSpec((None, 2, PEER_KEYS, PEER_KEYS), lambda i, h: (h, 0, 0, 0))],
        out_specs=[pl.BlockSpec((D_MODEL, tm), lambda i, h: (0, i)), stat, stat, stat, stat],
        out_shape=[jax.ShapeDtypeStruct((D_MODEL, t), BF16), stat_shape, stat_shape, stat_shape, stat_shape],
        scratch_shapes=[pltpu.VMEM((tm, D_MODEL), BF16), pltpu.VMEM((TOP_ROWS, tm), F32),
                        pltpu.VMEM((TOP_ROWS, tm), F32), pltpu.VMEM((CAND_ROWS, tm), F32),
                        pltpu.VMEM((TOP_ROWS, tm), F32)],
        compiler_params=_params(2),
        name="peer_route",
    )(x, g.reshape(1, D_MODEL), wq, keys)


PEER_CHUNK = 8 * PEER_KEYS
PEER_SUB = 2 * PEER_KEYS
C_ROWS = 32
V_SLICE = 256


def _peer_mix_kernel(ht_ref, u_ref, v_ref, thr_ref, cf_ref, s2_ref, p2_ref, x_ref, g_ref, y_ref, coef_t, c_scr):
    j = pl.program_id(1)
    last = pl.num_programs(1) - 1
    ec = u_ref.shape[0]

    def build_c(el):
        for rt in range(PEER_KEYS // C_ROWS):
            rows = slice(C_ROWS * rt, C_ROWS * (rt + 1))
            c = None
            for h in range(PEER_HEADS):
                term = jnp.where(s2_ref[h, rows, :] > thr_ref[h, el:el + 1, :],
                                 p2_ref[h, rows, :] * cf_ref[h, el:el + 1, :], 0.0)
                c = term if c is None else c + term
            c_scr[PEER_KEYS * el + C_ROWS * rt:PEER_KEYS * el + C_ROWS * (rt + 1), :] = c

    @pl.when(j == 0)
    def _():
        y_ref[...] = x_ref[...]
        for el in range(ec // PEER_KEYS):
            build_c(el)

    @pl.when(j > 0)
    def _():
        for sb in range(ec // PEER_SUB):
            rows = slice(PEER_SUB * sb, PEER_SUB * (sb + 1))
            act = _dot(u_ref[rows, :], ht_ref[...])
            coef_t[:, rows] = (c_scr[rows, :] * _gelu(act)).T.astype(BF16)
        for ns in range(D_MODEL // V_SLICE):
            cols = slice(V_SLICE * ns, V_SLICE * (ns + 1))
            y_ref[:, cols] += _dot(coef_t[...], v_ref[:, cols])
            build_c(ns)

    @pl.when(j == last)
    def _():
        y_ref[...] = _rms(y_ref[...], g_ref[...])


def peer_mix(ht, u, v, thr, cf, s2, p2, x, g, tm):
    t = x.shape[0]
    ec = PEER_CHUNK
    nc = PEER_EXPERTS // ec
    assert D_MODEL // V_SLICE == ec // PEER_KEYS
    rows = pl.BlockSpec((PEER_HEADS, ec // PEER_KEYS, tm), lambda i, j: (0, jnp.minimum(j, nc - 1), i))
    full = pl.BlockSpec((PEER_HEADS, PEER_KEYS, tm), lambda i, j: (0, 0, i))
    chunk = pl.BlockSpec((ec, D_MODEL), lambda i, j: (jnp.maximum(j - 1, 0), 0))
    return pl.pallas_call(
        _peer_mix_kernel,
        grid=(t // tm, nc + 1),
        in_specs=[pl.BlockSpec((D_MODEL, tm), lambda i, j: (0, i)), chunk, chunk,
                  rows, rows, full, full,
                  pl.BlockSpec((tm, D_MODEL), lambda i, j: (i, 0)),
                  pl.BlockSpec((1, D_MODEL), lambda i, j: (0, 0))],
        out_specs=pl.BlockSpec((tm, D_MODEL), lambda i, j: (i, 0)),
        out_shape=jax.ShapeDtypeStruct((t, D_MODEL), F32),
        scratch_shapes=[pltpu.VMEM((tm, ec), BF16), pltpu.VMEM((ec, tm), F32)],
        compiler_params=_params(2),
        name="peer_mix",
    )(ht, u, v, thr, cf, s2, p2, x, g.reshape(1, D_MODEL))


def _block_diag(w, per_tile):
    nb, bw, _ = w.shape
    w4 = w.reshape(nb // per_tile, per_tile, bw, bw)
    eye = jnp.eye(per_tile, dtype=w.dtype)
    tiles = w4[:, :, :, None, :] * eye[None, :, None, :, None]
    return tiles.reshape(nb // per_tile, per_tile * bw, per_tile * bw).astype(BF16)


def _rope_tables(pos):
    half = ROPE_DIM // 2
    inv = ROPE_THETA ** (-jnp.arange(half, dtype=F32) / half)
    ang = pos.astype(F32)[:, None] * inv[None, :]
    cos, sin = jnp.cos(ang), jnp.sin(ang)
    pad = jnp.zeros((pos.shape[0], LANES - ROPE_DIM), F32)
    return (jnp.concatenate([cos, cos, pad], axis=1), jnp.concatenate([-sin, sin, pad], axis=1))


def _prepare_weights(w_in, conv_w, conv_b, w_rec_a, b_rec_a, w_rec_x, b_rec_x, lru_lambda, w_o_lru,
                     g_q_a, w_q_b, g_kv_a, w_kv_b, w_o_mla, w_mem_k, w_mem_v, w_o_mem, w_out,
                     w_peer_query, peer_sub_keys, peer_u, peer_v):
    offs = [0] + np.cumsum([D_RNN, D_RNN, Q_LORA, KV_LORA + ROPE_DIM, MEM_HEADS * MEM_HEAD_DIM]).tolist() + [w_in.shape[1]]
    w_xr, w_gr, w_qa, w_kva, w_qm, w_gates = (w_in[:, a:b] for a, b in zip(offs[:-1], offs[1:]))
    pad = jnp.zeros((D_MODEL, P_WIDTH - OFF_KR - ROPE_DIM), w_in.dtype)
    w_p = jnp.concatenate([w_gates, w_xr, w_gr, w_qm, w_qa, w_kva, pad], axis=1).astype(BF16)
    row = lambda v: v.reshape(1, -1)
    lw = dict(cw=conv_w, cb=row(conv_b), wa=_block_diag(w_rec_a, 4), ba=row(b_rec_a),
              wx=_block_diag(w_rec_x, 4), bx=row(b_rec_x), lam=row(lru_lambda))
    w_pe = jnp.pad(w_q_b[:, :, NOPE_DIM:], ((0, 0), (0, 0), (0, LANES - ROPE_DIM)))
    mw = dict(gq=row(g_q_a), gkv=row(g_kv_a),
              wn=w_q_b[:, :, :NOPE_DIM].reshape(Q_LORA, MLA_HEADS * NOPE_DIM).astype(BF16),
              wp=w_pe.reshape(Q_LORA, MLA_HEADS * LANES).astype(BF16),
              wkn=jnp.transpose(w_kv_b[:, :, :NOPE_DIM], (1, 2, 0)).astype(BF16),
              wv=jnp.transpose(w_kv_b[:, :, NOPE_DIM:], (1, 0, 2)).astype(BF16))
    ow = dict(wl=w_o_lru.astype(BF16), wa=w_o_mla.astype(BF16), wm=w_o_mem.astype(BF16), wo=w_out.astype(BF16))
    w_mem = jnp.concatenate([w_mem_k.reshape(D_MODEL, -1), w_mem_v.reshape(D_MODEL, -1)], axis=1).astype(BF16)
    pw = dict(wq=w_peer_query.reshape(D_MODEL, -1).astype(BF16), keys=peer_sub_keys.astype(BF16),
              u=peer_u.astype(BF16), v=peer_v.astype(BF16))
    return w_p, lw, mw, ow, w_mem, pw


def _peer_and_norm(x2, g_ffn, g_final, pw, tm):
    ht, thr, cf, s2, p2 = peer_route(x2, g_ffn, pw["wq"], pw["keys"], tm)
    return peer_mix(ht, pw["u"], pw["v"], thr, cf, s2, p2, x2, g_final, tm)


def kernel(x_prompt, x_sample, cache_mla_ckv, cache_mla_krope, cache_mem_k, cache_mem_v, state_lru_h, state_lru_conv, page_table, mem_prompt, g_mix, w_in, conv_w, conv_b, w_rec_a, b_rec_a, w_rec_x, b_rec_x, lru_lambda, w_o_lru, g_q_a, w_q_b, g_kv_a, w_kv_b, w_o_mla, g_mem, w_mem_k, w_mem_v, w_o_mem, w_out, g_ffn, w_peer_query, peer_sub_keys, peer_u, peer_v, g_final):
    batch, seq, _ = x_prompt.shape
    dec_batch, dec_seq, _ = x_sample.shape
    past_len = page_table.shape[1] * PAGE_SIZE
    tp, ts = batch * seq, dec_batch * dec_seq
    tm = 512
    mem_width = MEM_HEADS * MEM_HEAD_DIM

    w_p, lw, mw, ow, w_mem, pw = _prepare_weights(
        w_in, conv_w, conv_b, w_rec_a, b_rec_a, w_rec_x, b_rec_x, lru_lambda, w_o_lru, g_q_a, w_q_b, g_kv_a,
        w_kv_b, w_o_mla, w_mem_k, w_mem_v, w_o_mem, w_out, w_peer_query, peer_sub_keys, peer_u, peer_v)

    xp = x_prompt.reshape(tp, D_MODEL)
    head_shape = (MEM_TOKENS, MEM_HEADS, MEM_HEAD_DIM)
    pp = norm_matmul(xp, g_mix, w_p, tm, PROJ_TN, "in_proj")
    mem_kv = norm_matmul(mem_prompt.reshape(batch * MEM_TOKENS, D_MODEL), g_mem, w_mem, tm, PROJ_TN, "mem_kv")
    mem_k_p = mem_kv[:, :mem_width].reshape(batch, MEM_TOKENS, mem_width)
    mem_v_p = mem_kv[:, mem_width:].reshape(batch, MEM_TOKENS, mem_width)
    y_lru_p, lru_h_p = lru_prompt(pp, batch, seq, lw)
    cos_p, sin_p = _rope_tables(jnp.tile(jnp.arange(seq), batch))
    q_p, kcat_p, ckv_p, kr_p = mla_prep(pp, cos_p, sin_p, mw, tm)
    o_mla_p = attn_prompt(q_p, kcat_p, mw["wv"], batch, seq)
    o_mem_p = mem_attn(pp, OFF_QM // mem_width, mem_k_p, mem_v_p, seq, tm)
    x2_p = merge(xp, pp, y_lru_p, o_mla_p, o_mem_p, ow, tm)
    y_p = _peer_and_norm(x2_p, g_ffn, g_final, pw, tm)

    xs = x_sample.reshape(ts, D_MODEL)
    ps = norm_matmul(xs, g_mix, w_p, tm, PROJ_TN, "in_proj")
    step_major = lambda a: jnp.transpose(a.reshape(dec_batch, dec_seq, -1), (1, 0, 2))
    xr_s = ps[:, OFF_XR:OFF_XR + D_RNN]
    y_lru_s, lru_h_s = lru_sample(step_major(xr_s), step_major(ps[:, OFF_GR:OFF_GR + D_RNN]),
                                  jnp.transpose(state_lru_conv, (1, 0, 2)), state_lru_h, lw)
    y_lru_s = jnp.transpose(y_lru_s, (1, 0, 2)).reshape(ts, D_RNN)
    cos_s, sin_s = _rope_tables(jnp.tile(past_len + jnp.arange(dec_seq), dec_batch))
    q_s, kcat_s, ckv_s, kr_s = mla_prep(ps, cos_s, sin_s, mw, tm)
    q_rows = jnp.transpose(q_s.reshape(MLA_HEADS, dec_batch, dec_seq, QK_PAD), (1, 0, 2, 3))
    q_rows = q_rows.reshape(dec_batch, MLA_HEADS * dec_seq, QK_PAD)
    k_new = jnp.pad(kcat_s.reshape(dec_batch, dec_seq, QK_PAD), ((0, 0), (0, PAGE_SIZE - dec_seq), (0, 0)))
    o_lat = attn_sample(page_table, q_rows, k_new, cache_mla_ckv, jnp.swapaxes(cache_mla_krope, 1, 2))
    o_lat = jnp.transpose(o_lat.reshape(dec_batch, MLA_HEADS, dec_seq, KV_LORA), (1, 0, 2, 3))
    o_mla_s = vproj(o_lat.reshape(MLA_HEADS, ts, KV_LORA).astype(BF16), mw["wv"])
    q_pad_rows = 16
    qm_s = jnp.pad(ps[:, OFF_QM:OFF_QM + mem_width].reshape(dec_batch, dec_seq, mem_width),
                   ((0, 0), (0, q_pad_rows - dec_seq), (0, 0)))
    o_mem_s = mem_attn_cache(qm_s, cache_mem_k, cache_mem_v)[:, :dec_seq].reshape(ts, mem_width)
    x2_s = merge(xs, ps, y_lru_s, o_mla_s, o_mem_s, ow, tm)
    y_s = _peer_and_norm(x2_s, g_ffn, g_final, pw, tm)

    conv_tail = CONV_W - 1
    return (y_p.reshape(batch, seq, D_MODEL), y_s.reshape(dec_batch, dec_seq, D_MODEL),
            ckv_p.reshape(batch, seq, KV_LORA), kr_p[:, :ROPE_DIM].reshape(batch, seq, ROPE_DIM),
            mem_k_p.reshape((batch,) + head_shape), mem_v_p.reshape((batch,) + head_shape),
            lru_h_p, pp.reshape(batch, seq, P_WIDTH)[:, seq - conv_tail:, OFF_XR:OFF_XR + D_RNN],
            ckv_s.reshape(dec_batch, dec_seq, KV_LORA), kr_s[:, :ROPE_DIM].reshape(dec_batch, dec_seq, ROPE_DIM),
            lru_h_s, xr_s.reshape(dec_batch, dec_seq, D_RNN)[:, dec_seq - conv_tail:])
```

```python
import functools
import math

import jax
import jax.numpy as jnp
import numpy as np
from jax import lax
from jax.experimental import pallas as pl
from jax.experimental.pallas import tpu as pltpu

F32 = jnp.float32
BF16 = jnp.bfloat16

D_MODEL = 2048
EPS = 1e-6
D_RNN = 1024
LRU_BLOCKS = 16
CONV_W = 4
LRU_C = 8.0
MLA_HEADS = 16
Q_LORA = 512
KV_LORA = 256
NOPE_DIM = 128
ROPE_DIM = 64
V_DIM = 128
ROPE_THETA = 10000.0
MEM_TOKENS = 256
MEM_HEADS = 4
MEM_HEAD_DIM = 256
PEER_HEADS = 8
PEER_KEYS = 128
PEER_EXPERTS = PEER_KEYS * PEER_KEYS
PEER_TOPK = 16
PAGE_SIZE = 128

LANES = 128
QK_PAD = KV_LORA + LANES
ATTN_SCALE = (NOPE_DIM + ROPE_DIM) ** -0.5
ATTN_SCALE_LOG2E = ATTN_SCALE * math.log2(math.e)
MEM_SCALE = MEM_HEAD_DIM ** -0.5
NEG_INF = float("-inf")

OFF_GATES = 0
OFF_XR = 3 * D_MODEL
OFF_GR = OFF_XR + D_RNN
OFF_QM = OFF_GR + D_RNN
OFF_QA = OFF_QM + MEM_HEADS * MEM_HEAD_DIM
OFF_CKV = OFF_QA + Q_LORA
OFF_KR = OFF_CKV + KV_LORA
P_WIDTH = OFF_KR + 2 * LANES
PROJ_TN = 2048

VMEM_LIMIT = 56 * 1024 * 1024


def _params(n_grid):
    return pltpu.CompilerParams(dimension_semantics=("arbitrary",) * n_grid, vmem_limit_bytes=VMEM_LIMIT)


def _rms(x, g):
    ms = jnp.mean(x * x, axis=-1, keepdims=True)
    return x * lax.rsqrt(ms + EPS) * g


def _gelu(x):
    c1 = math.sqrt(2.0 / math.pi)
    t = jnp.tanh(x * (c1 + (0.044715 * c1) * (x * x)))
    hx = 0.5 * x
    return hx + hx * t


def _dot(a, b):
    return jnp.dot(a, b, preferred_element_type=F32)


def _dot_nt(a, b):
    return lax.dot_general(a, b, (((1,), (1,)), ((), ())), preferred_element_type=F32)


def _dot_tn(a, b):
    return lax.dot_general(a, b, (((0,), (0,)), ((), ())), preferred_element_type=F32)


def _norm_matmul_kernel(x_ref, g_ref, w_ref, o_ref, h_scr):
    @pl.when(pl.program_id(1) == 0)
    def _():
        h_scr[...] = _rms(x_ref[...], g_ref[...]).astype(BF16)

    o_ref[...] = _dot(h_scr[...], w_ref[...])


def norm_matmul(x, g, w, tm, tn, name):
    t, d = x.shape
    n = w.shape[1]
    return pl.pallas_call(
        _norm_matmul_kernel,
        grid=(t // tm, n // tn),
        in_specs=[pl.BlockSpec((tm, d), lambda i, j: (i, 0)),
                  pl.BlockSpec((1, d), lambda i, j: (0, 0)),
                  pl.BlockSpec((d, tn), lambda i, j: (0, j))],
        out_specs=pl.BlockSpec((tm, tn), lambda i, j: (i, j)),
        out_shape=jax.ShapeDtypeStruct((t, n), F32),
        scratch_shapes=[pltpu.VMEM((tm, d), BF16)],
        compiler_params=_params(2),
        name=name,
    )(x, g.reshape(1, d), w)


def _lru_gates(xc, wa_ref, ba, wx_ref, bx, lam):
    xb = xc.astype(BF16)
    nk = wa_ref.shape[0]
    wk = D_RNN // nk
    ra = [_dot(xb[:, wk * k:wk * (k + 1)], wa_ref[k]) for k in range(nk)]
    ix = [_dot(xb[:, wk * k:wk * (k + 1)], wx_ref[k]) for k in range(nk)]
    r = jax.nn.sigmoid(jnp.concatenate(ra, axis=1) + ba)
    i = jax.nn.sigmoid(jnp.concatenate(ix, axis=1) + bx)
    neg_lam = -lam
    softplus = jnp.maximum(neg_lam, 0.0) + jnp.log1p(jnp.exp(-jnp.abs(neg_lam)))
    log_a = -LRU_C * r * softplus
    a = jnp.exp(log_a)
    u = jnp.sqrt(-jnp.tanh(log_a) * (a * a + 1.0)) * (i * xc)
    return a, u


def _lru_prompt_kernel(xr_ref, gr_ref, cw_ref, cb_ref, wa_ref, ba_ref, wx_ref, bx_ref, lam_ref,
                       y_ref, hl_ref, xpad, a_scr, u_scr, hs_scr, h_scr):
    tm = xr_ref.shape[0]

    @pl.when(pl.program_id(1) == 0)
    def _():
        xpad[0:8, :] = jnp.zeros((8, D_RNN), F32)
        h_scr[...] = jnp.zeros_like(h_scr)

    xr = xr_ref[...]
    xpad[8:8 + tm, :] = xr
    cw = cw_ref[...]
    xc = cb_ref[...] + (xpad[5:5 + tm, :] * cw[0:1] + xpad[6:6 + tm, :] * cw[1:2]
                        + xpad[7:7 + tm, :] * cw[2:3] + xr * cw[3:4])
    xpad[0:8, :] = xr[tm - 8:tm, :]
    a, u = _lru_gates(xc, wa_ref, ba_ref[...], wx_ref, bx_ref[...], lam_ref[...])
    a_scr[...] = a
    u_scr[...] = u

    row = lax.broadcasted_iota(jnp.int32, (8, D_RNN), 0)

    def body(g, h):
        r0 = pl.multiple_of(g * 8, 8)
        a8 = a_scr[pl.ds(r0, 8), :]
        u8 = u_scr[pl.ds(r0, 8), :]
        for k in (1, 2, 4):
            has_prev = row >= k
            a_prev = jnp.where(has_prev, pltpu.roll(a8, k, 0), 1.0)
            u_prev = jnp.where(has_prev, pltpu.roll(u8, k, 0), 0.0)
            u8 = u8 + a8 * u_prev
            a8 = a8 * a_prev
        hs = u8 + a8 * h
        hs_scr[pl.ds(r0, 8), :] = hs
        return hs[7:8, :]

    h = lax.fori_loop(0, tm // 8, body, h_scr[0:1, :], unroll=4)
    h_scr[0:1, :] = h
    y_ref[...] = (hs_scr[...] * _gelu(gr_ref[...])).astype(BF16)
    hl_ref[...] = h.reshape(1, 1, D_RNN)


def lru_prompt(p, batch, seq, lw, tm=256):
    nc = seq // tm
    full = lambda shape: pl.BlockSpec(shape, lambda b, c: (0,) * len(shape))
    y, hl = pl.pallas_call(
        _lru_prompt_kernel,
        grid=(batch, nc),
        in_specs=[pl.BlockSpec((tm, D_RNN), lambda b, c: (b * nc + c, OFF_XR // D_RNN)),
                  pl.BlockSpec((tm, D_RNN), lambda b, c: (b * nc + c, OFF_GR // D_RNN)),
                  full((CONV_W, D_RNN)), full((1, D_RNN)),
                  full(lw["wa"].shape), full((1, D_RNN)), full(lw["wx"].shape), full((1, D_RNN)),
                  full((1, D_RNN))],
        out_specs=[pl.BlockSpec((tm, D_RNN), lambda b, c: (b * nc + c, 0)),
                   pl.BlockSpec((1, 1, D_RNN), lambda b, c: (b, 0, 0))],
        out_shape=[jax.ShapeDtypeStruct((batch * seq, D_RNN), BF16),
                   jax.ShapeDtypeStruct((batch, 1, D_RNN), F32)],
        scratch_shapes=[pltpu.VMEM((tm + 8, D_RNN), F32), pltpu.VMEM((tm, D_RNN), F32),
                        pltpu.VMEM((tm, D_RNN), F32), pltpu.VMEM((tm, D_RNN), F32),
                        pltpu.VMEM((8, D_RNN), F32)],
        compiler_params=_params(2),
        name="lru_prompt",
    )(p, p, lw["cw"], lw["cb"], lw["wa"], lw["ba"], lw["wx"], lw["bx"], lw["lam"])
    return y, hl.reshape(batch, D_RNN)


def _lru_sample_kernel(xs_ref, gs_ref, c0_ref, h0_ref, cw_ref, cb_ref, wa_ref, ba_ref, wx_ref, bx_ref, lam_ref,
                       y_ref, hl_ref):
    steps = xs_ref.shape[0]
    cw = cw_ref[...]
    xcat = [c0_ref[k] for k in range(CONV_W - 1)] + [xs_ref[s] for s in range(steps)]
    h = h0_ref[...]
    for s in range(steps):
        xc = cb_ref[...] + (xcat[s] * cw[0:1] + xcat[s + 1] * cw[1:2] + xcat[s + 2] * cw[2:3]
                            + xcat[s + 3] * cw[3:4])
        a, u = _lru_gates(xc, wa_ref, ba_ref[...], wx_ref, bx_ref[...], lam_ref[...])
        h = a * h + u
        y_ref[s] = (h * _gelu(gs_ref[s])).astype(BF16)
    hl_ref[...] = h


def lru_sample(xs, gs, c0, h0, lw):
    steps, batch, _ = xs.shape
    return pl.pallas_call(
        _lru_sample_kernel,
        out_shape=[jax.ShapeDtypeStruct((steps, batch, D_RNN), BF16),
                   jax.ShapeDtypeStruct((batch, D_RNN), F32)],
        compiler_params=pltpu.CompilerParams(vmem_limit_bytes=VMEM_LIMIT),
        name="lru_sample",
    )(xs, gs, c0, h0, lw["cw"], lw["cb"], lw["wa"], lw["ba"], lw["wx"], lw["bx"], lw["lam"])


def _rope(x, cos, sin_signed):
    lane = lax.broadcasted_iota(jnp.int32, x.shape, 1)
    half = ROPE_DIM // 2
    rot = jnp.where(lane < half, pltpu.roll(x, LANES - half, 1), pltpu.roll(x, half, 1))
    return x * cos + rot * sin_signed


def _mla_prep_kernel(qa_ref, ckv_ref, kr_ref, cos_ref, sin_ref, gq_ref, gkv_ref, wn_ref, wp_ref, wkn_ref,
                     q_ref, kcat_ref, ckv_out, kr_out):
    cos = cos_ref[...]
    sin = sin_ref[...]
    qn = _rms(qa_ref[...], gq_ref[...]).astype(BF16)
    q_nope = _dot(qn, wn_ref[...]).astype(BF16)
    q_pe = _dot(qn, wp_ref[...])
    for h in range(MLA_HEADS):
        q_abs = _dot(q_nope[:, NOPE_DIM * h:NOPE_DIM * (h + 1)], wkn_ref[h])
        q_ref[h, :, 0:KV_LORA] = q_abs.astype(BF16)
        q_ref[h, :, KV_LORA:QK_PAD] = _rope(q_pe[:, LANES * h:LANES * (h + 1)], cos, sin).astype(BF16)
    ckv = _rms(ckv_ref[...], gkv_ref[...])
    kr = _rope(kr_ref[...], cos, sin)
    ckv_out[...] = ckv
    kr_out[...] = kr
    kcat_ref[:, 0:KV_LORA] = ckv.astype(BF16)
    kcat_ref[:, KV_LORA:QK_PAD] = kr.astype(BF16)


def mla_prep(p, cos, sin, mw, tm):
    t = p.shape[0]
    full = lambda shape: pl.BlockSpec(shape, lambda i: (0,) * len(shape))
    return pl.pallas_call(
        _mla_prep_kernel,
        grid=(t // tm,),
        in_specs=[pl.BlockSpec((tm, Q_LORA), lambda i: (i, OFF_QA // Q_LORA)),
                  pl.BlockSpec((tm, KV_LORA), lambda i: (i, OFF_CKV // KV_LORA)),
                  pl.BlockSpec((tm, LANES), lambda i: (i, OFF_KR // LANES)),
                  pl.BlockSpec((tm, LANES), lambda i: (i, 0)),
                  pl.BlockSpec((tm, LANES), lambda i: (i, 0)),
                  full((1, Q_LORA)), full((1, KV_LORA)),
                  full(mw["wn"].shape), full(mw["wp"].shape), full(mw["wkn"].shape)],
        out_specs=[pl.BlockSpec((MLA_HEADS, tm, QK_PAD), lambda i: (0, i, 0)),
                   pl.BlockSpec((tm, QK_PAD), lambda i: (i, 0)),
                   pl.BlockSpec((tm, KV_LORA), lambda i: (i, 0)),
                   pl.BlockSpec((tm, LANES), lambda i: (i, 0))],
        out_shape=[jax.ShapeDtypeStruct((MLA_HEADS, t, QK_PAD), BF16),
                   jax.ShapeDtypeStruct((t, QK_PAD), BF16),
                   jax.ShapeDtypeStruct((t, KV_LORA), F32),
                   jax.ShapeDtypeStruct((t, LANES), F32)],
        compiler_params=_params(1),
        name="mla_prep",
    )(p, p, p, cos, sin, mw["gq"], mw["gkv"], mw["wn"], mw["wp"], mw["wkn"])


ATTN_HEAD_GROUP = 4

def _attn_prompt_kernel(q_ref, k_ref, wv_ref, o_ref, m_scr, l_scr, acc_scr):
    qi = pl.program_id(1)
    ki = pl.program_id(2)
    heads, tq, _ = q_ref.shape
    tk = k_ref.shape[0]

    @pl.when(ki == 0)
    def _():
        m_scr[...] = jnp.full_like(m_scr, NEG_INF)
        l_scr[...] = jnp.zeros_like(l_scr)
        acc_scr[...] = jnp.zeros_like(acc_scr)

    def lanes(x, width):
        return jnp.concatenate([x] * (width // LANES), axis=1)

    def block(masked):
        k = k_ref[...]
        v = k[:, 0:KV_LORA]
        g = ATTN_HEAD_GROUP
        if masked:
            col = lax.broadcasted_iota(jnp.int32, (g, tq, tk), 2).reshape(g * tq, tk)
            row = lax.broadcasted_iota(jnp.int32, (g, tq, tk), 1).reshape(g * tq, tk)
            visible = col <= row
        for hg in range(heads // g):
            rs = slice(hg * g * tq, (hg + 1) * g * tq)
            s = _dot_nt(q_ref[hg * g:(hg + 1) * g].reshape(g * tq, QK_PAD), k)
            if masked:
                s = jnp.where(visible, s, NEG_INF)
            m_old = m_scr[rs, :]
            m_new = jnp.maximum(m_old, jnp.max(s, axis=-1, keepdims=True))
            alpha = jnp.exp2((m_old - m_new) * ATTN_SCALE_LOG2E)
            p = jnp.exp2((s - lanes(m_new, tk)) * ATTN_SCALE_LOG2E)
            l_scr[rs, :] = alpha * l_scr[rs, :] + jnp.sum(p, axis=-1, keepdims=True)
            acc_scr[rs, :] = lanes(alpha, KV_LORA) * acc_scr[rs, :] + _dot(p.astype(BF16), v)
            m_scr[rs, :] = m_new

    @pl.when(ki < qi)
    def _():
        block(False)

    @pl.when(ki == qi)
    def _():
        block(True)
        for h in range(heads):
            rs = slice(h * tq, (h + 1) * tq)
            o = (acc_scr[rs, :] / lanes(l_scr[rs, :], KV_LORA)).astype(BF16)
            o_ref[:, V_DIM * h:V_DIM * (h + 1)] = _dot(o, wv_ref[h]).astype(BF16)


def attn_prompt(q, kcat, wv, batch, seq, tq=256):
    nq = seq // tq
    return pl.pallas_call(
        _attn_prompt_kernel,
        grid=(batch, nq, nq),
        in_specs=[pl.BlockSpec((MLA_HEADS, tq, QK_PAD), lambda b, i, j: (0, b * nq + i, 0)),
                  pl.BlockSpec((tq, QK_PAD), lambda b, i, j: (b * nq + jnp.minimum(i, j), 0)),
                  pl.BlockSpec(wv.shape, lambda b, i, j: (0, 0, 0))],
        out_specs=pl.BlockSpec((tq, MLA_HEADS * V_DIM), lambda b, i, j: (b * nq + i, 0)),
        out_shape=jax.ShapeDtypeStruct((batch * seq, MLA_HEADS * V_DIM), BF16),
        scratch_shapes=[pltpu.VMEM((MLA_HEADS * tq, LANES), F32), pltpu.VMEM((MLA_HEADS * tq, LANES), F32),
                        pltpu.VMEM((MLA_HEADS * tq, KV_LORA), F32)],
        compiler_params=_params(3),
        name="attn_prompt",
    )(q, kcat, wv)


PAGES_PER_CHUNK = 32


def _attn_sample_kernel(pt_ref, q_ref, kn_ref, ckv_hbm, kr_hbm, o_ref, ckv_buf, kr_buf, sem):
    b = pl.program_id(0)
    nb = pl.num_programs(0)
    n_pages = ckv_buf.shape[1]
    slot = b % 2

    def page_copies(bb, sl):
        out = []
        for pg in range(n_pages):
            page = pt_ref[bb, pg]
            out.append(pltpu.make_async_copy(ckv_hbm.at[page], ckv_buf.at[sl, pg], sem.at[0, sl]))
            out.append(pltpu.make_async_copy(kr_hbm.at[page], kr_buf.at[sl, pg], sem.at[1, sl]))
        return out

    @pl.when(b == 0)
    def _():
        for c in page_copies(0, 0):
            c.start()

    @pl.when(b + 1 < nb)
    def _():
        for c in page_copies(b + 1, 1 - slot):
            c.start()

    for c in page_copies(b, slot):
        c.wait()

    q = q_ref[...]
    rows = q.shape[0]
    qa = q[:, 0:KV_LORA]
    qp = q[:, KV_LORA:KV_LORA + ROPE_DIM]
    m = jnp.full((rows, 1), NEG_INF, F32)
    l = jnp.zeros((rows, 1), F32)
    acc = jnp.zeros((rows, KV_LORA), F32)

    def online(s, v, m, l, acc):
        m_new = jnp.maximum(m, jnp.max(s, axis=-1, keepdims=True))
        alpha = jnp.exp(m - m_new)
        p = jnp.exp(s - m_new)
        l = alpha * l + jnp.sum(p, axis=-1, keepdims=True)
        acc = alpha * acc + _dot(p.astype(BF16), v)
        return m_new, l, acc

    ck = PAGES_PER_CHUNK
    for c in range(n_pages // ck):
        kc = ckv_buf[slot, ck * c:ck * (c + 1)].reshape(ck * PAGE_SIZE, KV_LORA).astype(BF16)
        kr_t = jnp.concatenate([kr_buf[slot, ck * c + pg] for pg in range(ck)], axis=1).astype(BF16)
        s = (_dot_nt(qa, kc) + _dot(qp, kr_t)) * ATTN_SCALE
        m, l, acc = online(s, kc, m, l, acc)

    kn = kn_ref[...]
    s = _dot_nt(q, kn) * ATTN_SCALE
    steps = rows // MLA_HEADS
    q_step = lax.broadcasted_iota(jnp.int32, (MLA_HEADS, steps, kn.shape[0]), 1).reshape(rows, kn.shape[0])
    k_step = lax.broadcasted_iota(jnp.int32, (rows, kn.shape[0]), 1)
    s = jnp.where(k_step <= q_step, s, NEG_INF)
    m, l, acc = online(s, kn[:, 0:KV_LORA], m, l, acc)
    o_ref[...] = acc / l


def attn_sample(page_table, q, kn, cache_ckv, cache_kr_t):
    nb, rows, _ = q.shape
    n_pages = page_table.shape[1]
    grid_spec = pltpu.PrefetchScalarGridSpec(
        num_scalar_prefetch=1,
        grid=(nb,),
        in_specs=[pl.BlockSpec((None, rows, QK_PAD), lambda b, pt: (b, 0, 0)),
                  pl.BlockSpec((None, kn.shape[1], QK_PAD), lambda b, pt: (b, 0, 0)),
                  pl.BlockSpec(memory_space=pl.ANY),
                  pl.BlockSpec(memory_space=pl.ANY)],
        out_specs=pl.BlockSpec((None, rows, KV_LORA), lambda b, pt: (b, 0, 0)),
        scratch_shapes=[pltpu.VMEM((2, n_pages, PAGE_SIZE, KV_LORA), F32),
                        pltpu.VMEM((2, n_pages, ROPE_DIM, PAGE_SIZE), F32),
                        pltpu.SemaphoreType.DMA((2, 2))],
    )
    return pl.pallas_call(
        _attn_sample_kernel,
        grid_spec=grid_spec,
        out_shape=jax.ShapeDtypeStruct((nb, rows, KV_LORA), F32),
        compiler_params=_params(1),
        name="attn_sample",
    )(page_table, q, kn, cache_ckv, cache_kr_t)


def _vproj_kernel(o_ref, wv_ref, out_ref):
    out_ref[...] = _dot(o_ref[...], wv_ref[...]).astype(BF16)


def vproj(o_lat, wv):
    heads, t, _ = o_lat.shape
    return pl.pallas_call(
        _vproj_kernel,
        grid=(heads,),
        in_specs=[pl.BlockSpec((None, t, KV_LORA), lambda h: (h, 0, 0)),
                  pl.BlockSpec((None, KV_LORA, V_DIM), lambda h: (h, 0, 0))],
        out_specs=pl.BlockSpec((t, V_DIM), lambda h: (0, h)),
        out_shape=jax.ShapeDtypeStruct((t, heads * V_DIM), BF16),
        compiler_params=_params(1),
        name="vproj",
    )(o_lat, wv)


def _mem_heads(q, head_k, head_v, o_ref):
    q = q.astype(BF16)
    for h in range(MEM_HEADS):
        sl = slice(MEM_HEAD_DIM * h, MEM_HEAD_DIM * (h + 1))
        s = _dot_nt(q[:, sl], head_k(h).astype(BF16)) * MEM_SCALE
        e = jnp.exp(s - jnp.max(s, axis=-1, keepdims=True))
        p = e / jnp.sum(e, axis=-1, keepdims=True)
        o_ref[:, sl] = _dot(p.astype(BF16), head_v(h).astype(BF16)).astype(BF16)


def _mem_attn_kernel(q_ref, k_ref, v_ref, o_ref):
    _mem_heads(q_ref[...], lambda h: k_ref[:, MEM_HEAD_DIM * h:MEM_HEAD_DIM * (h + 1)],
               lambda h: v_ref[:, MEM_HEAD_DIM * h:MEM_HEAD_DIM * (h + 1)], o_ref)


def mem_attn(q, q_col_block, mem_k, mem_v, rows_per_batch, tm):
    batch = mem_k.shape[0]
    width = MEM_HEADS * MEM_HEAD_DIM
    nc = rows_per_batch // tm
    kv_spec = pl.BlockSpec((None, MEM_TOKENS, width), lambda b, c: (b, 0, 0))
    return pl.pallas_call(
        _mem_attn_kernel,
        grid=(batch, nc),
        in_specs=[pl.BlockSpec((tm, width), lambda b, c: (b * nc + c, q_col_block)), kv_spec, kv_spec],
        out_specs=pl.BlockSpec((tm, width), lambda b, c: (b * nc + c, 0)),
        out_shape=jax.ShapeDtypeStruct((batch * rows_per_batch, width), BF16),
        compiler_params=_params(2),
        name="mem_attn",
    )(q, mem_k, mem_v)


def _mem_attn_cache_kernel(q_ref, k_hbm, v_hbm, o_ref, k_buf, v_buf, sem):
    b = pl.program_id(0)
    nb = pl.num_programs(0)
    slot = b % 2

    def head_copies(bb, sl):
        out = []
        for h in range(MEM_HEADS):
            out.append(pltpu.make_async_copy(k_hbm.at[bb, :, h, :], k_buf.at[sl, h], sem.at[0, sl]))
            out.append(pltpu.make_async_copy(v_hbm.at[bb, :, h, :], v_buf.at[sl, h], sem.at[1, sl]))
        return out

    @pl.when(b == 0)
    def _():
        for c in head_copies(0, 0):
            c.start()

    @pl.when(b + 1 < nb)
    def _():
        for c in head_copies(b + 1, 1 - slot):
            c.start()

    for c in head_copies(b, slot):
        c.wait()

    _mem_heads(q_ref[...], lambda h: k_buf[slot, h], lambda h: v_buf[slot, h], o_ref)


def mem_attn_cache(q, cache_k, cache_v):
    batch, rows, width = q.shape
    buf = pltpu.VMEM((2, MEM_HEADS, MEM_TOKENS, MEM_HEAD_DIM), F32)
    return pl.pallas_call(
        _mem_attn_cache_kernel,
        grid=(batch,),
        in_specs=[pl.BlockSpec((None, rows, width), lambda b: (b, 0, 0)),
                  pl.BlockSpec(memory_space=pl.ANY), pl.BlockSpec(memory_space=pl.ANY)],
        out_specs=pl.BlockSpec((None, rows, width), lambda b: (b, 0, 0)),
        out_shape=jax.ShapeDtypeStruct((batch, rows, width), BF16),
        scratch_shapes=[buf, buf, pltpu.SemaphoreType.DMA((2, 2))],
        compiler_params=_params(1),
        name="mem_attn_cache",
    )(q, cache_k, cache_v)


def _merge_kernel(x_ref, gl_ref, ga_ref, gm_ref, yl_ref, ya_ref, ym_ref, wl_ref, wa_ref, wm_ref, wo_ref,
                  o_ref, acc):
    j = pl.program_id(1)

    @pl.when(j == 0)
    def _():
        acc[...] = jnp.zeros_like(acc)

    merged = (jax.nn.sigmoid(gl_ref[...]) * _dot(yl_ref[...], wl_ref[...])
              + jax.nn.sigmoid(ga_ref[...]) * _dot(ya_ref[...], wa_ref[...])
              + jax.nn.sigmoid(gm_ref[...]) * _dot(ym_ref[...], wm_ref[...]))
    acc[...] += _dot(merged.astype(BF16), wo_ref[...])

    @pl.when(j == pl.num_programs(1) - 1)
    def _():
        o_ref[...] = x_ref[...] + acc[...]


def merge(x, p, y_lru, o_mla, o_mem, ow, tm, tn=512):
    t = x.shape[0]
    nj = D_MODEL // tn
    gate = lambda k: pl.BlockSpec((tm, tn), lambda i, j: (i, (OFF_GATES + k * D_MODEL) // tn + j))
    act = lambda a: pl.BlockSpec((tm, a.shape[1]), lambda i, j: (i, 0))
    wcol = lambda w: pl.BlockSpec((w.shape[0], tn), lambda i, j: (0, j))
    return pl.pallas_call(
        _merge_kernel,
        grid=(t // tm, nj),
        in_specs=[pl.BlockSpec((tm, D_MODEL), lambda i, j: (i, 0)), gate(0), gate(1), gate(2),
                  act(y_lru), act(o_mla), act(o_mem),
                  wcol(ow["wl"]), wcol(ow["wa"]), wcol(ow["wm"]),
                  pl.BlockSpec((tn, D_MODEL), lambda i, j: (j, 0))],
        out_specs=pl.BlockSpec((tm, D_MODEL), lambda i, j: (i, 0)),
        out_shape=jax.ShapeDtypeStruct((t, D_MODEL), F32),
        scratch_shapes=[pltpu.VMEM((tm, D_MODEL), F32)],
        compiler_params=_params(2),
        name="merge",
    )(x, p, p, p, y_lru, o_mla, o_mem, ow["wl"], ow["wa"], ow["wm"], ow["wo"])


N_TOP = PEER_TOPK + 1
CAND_PAIRS = [(i, j) for i in range(N_TOP) for j in range(N_TOP) if (i + 1) * (j + 1) <= N_TOP]
CAND_ROWS = -(-len(CAND_PAIRS) // 8) * 8
TOP_ROWS = -(-N_TOP // 8) * 8


def _top_values(s, n, out_ref):
    for r in range(n):
        m = jnp.max(s, axis=0, keepdims=True)
        out_ref[r:r + 1, :] = m
        s = jnp.where(s == m, NEG_INF, s)


def _odd_even_merge_sort_pairs(n):
    pairs = []

    def merge(lo, hi, r):
        step = r * 2
        if step < hi - lo:
            merge(lo, hi, step)
            merge(lo + r, hi, step)
            pairs.extend((i, i + r) for i in range(lo + r, hi - r, step))
        else:
            pairs.append((lo, lo + r))

    def sort(lo, hi):
        if hi - lo >= 1:
            mid = lo + (hi - lo) // 2
            sort(lo, mid)
            sort(mid + 1, hi)
            merge(lo, hi, 1)

    sort(0, n - 1)
    return pairs


SUBLANES = 8
KEY_TILES = PEER_KEYS // SUBLANES
SORT_PAIRS = _odd_even_merge_sort_pairs(KEY_TILES)


def _exchange(t, i, j):
    t[i], t[j] = jnp.maximum(t[i], t[j]), jnp.minimum(t[i], t[j])


def _top_keys(s_ref, out_ref):
    for lt in range(s_ref.shape[1] // LANES):
        lanes = slice(LANES * lt, LANES * (lt + 1))
        t = [s_ref[SUBLANES * k:SUBLANES * (k + 1), lanes] for k in range(KEY_TILES)]
        for i, j in SORT_PAIRS:
            _exchange(t, i, j)
        shift = SUBLANES // 2
        while shift:
            t = [jnp.maximum(t[k], pltpu.roll(t[KEY_TILES - 1 - k], shift, 0)) for k in range(KEY_TILES)]
            half = KEY_TILES // 2
            while half:
                for k in range(KEY_TILES):
                    if k & half == 0:
                        _exchange(t, k, k + half)
                half //= 2
            shift //= 2
        for r in range(PEER_TOPK):
            out_ref[r:r + 1, lanes] = t[r][0:1, :]
        rest = None
        for k in range(KEY_TILES):
            tile = s_ref[SUBLANES * k:SUBLANES * (k + 1), lanes]
            below = jnp.where(tile < t[PEER_TOPK - 1], tile, NEG_INF)
            rest = below if rest is None else jnp.maximum(rest, below)
        out_ref[PEER_TOPK:PEER_TOPK + 1, lanes] = jnp.max(rest, axis=0, keepdims=True)


def _peer_route_kernel(x_ref, g_ref, wq_ref, k_ref, ht_ref, thr_ref, cf_ref, s2_ref, p2_ref,
                       hb, top_a, top_b, cand, top_c, s1_scr):
    @pl.when(pl.program_id(1) == 0)
    def _():
        hf = _rms(x_ref[...], g_ref[...])
        hb[...] = hf.astype(BF16)
        ht_ref[...] = hf.T.astype(BF16)

    qb = _dot(hb[...], wq_ref[...]).astype(BF16)
    s1_scr[...] = _dot_nt(k_ref[0], qb[:, 0:PEER_KEYS])
    s2_ref[...] = _dot_nt(k_ref[1], qb[:, PEER_KEYS:2 * PEER_KEYS])
    _top_keys(s1_scr, top_a)
    _top_keys(s2_ref, top_b)
    s1 = s1_scr[...]
    s2 = s2_ref[...]
    cand[...] = jnp.full(cand.shape, NEG_INF, F32)
    for n, (i, j) in enumerate(CAND_PAIRS):
        cand[n:n + 1, :] = top_a[i:i + 1, :] + top_b[j:j + 1, :]
    _top_values(cand[...], N_TOP, top_c)
    c = top_c[...]
    c_max = c[0:1, :]
    z = jnp.sum(jnp.exp(c[0:PEER_TOPK, :] - c_max), axis=0, keepdims=True)
    tau = 0.5 * (c[PEER_TOPK - 1:PEER_TOPK, :] + c[PEER_TOPK:PEER_TOPK + 1, :])
    thr_ref[...] = tau - s1
    cf_ref[...] = jnp.exp(s1 - top_a[0:1, :]) / z
    p2_ref[...] = jnp.exp(s2 - top_b[0:1, :])


def peer_route(x, g, wq, keys, tm):
    t = x.shape[0]
    stat = pl.BlockSpec((None, PEER_KEYS, tm), lambda i, h: (h, 0, i))
    stat_shape = jax.ShapeDtypeStruct((PEER_HEADS, PEER_KEYS, t), F32)
    return pl.pallas_call(
        _peer_route_kernel,
        grid=(t // tm, PEER_HEADS),
        in_specs=[pl.BlockSpec((tm, D_MODEL), lambda i, h: (i, 0)),
                  pl.BlockSpec((1, D_MODEL), lambda i, h: (0, 0)),
                  pl.BlockSpec((D_MODEL, 2 * PEER_KEYS), lambda i, h: (0, h)),
                  pl.BlockSpec((None, 2, PEER_KEYS, PEER_KEYS), lambda i, h: (h, 0, 0, 0))],
        out_specs=[pl.BlockSpec((D_MODEL, tm), lambda i, h: (0, i)), stat, stat, stat, stat],
        out_shape=[jax.ShapeDtypeStruct((D_MODEL, t), BF16), stat_shape, stat_shape, stat_shape, stat_shape],
        scratch_shapes=[pltpu.VMEM((tm, D_MODEL), BF16), pltpu.VMEM((TOP_ROWS, tm), F32),
                        pltpu.VMEM((TOP_ROWS, tm), F32), pltpu.VMEM((CAND_ROWS, tm), F32),
                        pltpu.VMEM((TOP_ROWS, tm), F32), pltpu.VMEM((PEER_KEYS, tm), F32)],
        compiler_params=_params(2),
        name="peer_route",
    )(x, g.reshape(1, D_MODEL), wq, keys)


PEER_CHUNK = 8 * PEER_KEYS
PEER_SUB = 2 * PEER_KEYS
C_ROWS = 32
V_SLICE = 256


def _peer_mix_kernel(ht_ref, u_ref, v_ref, thr_ref, cf_ref, s2_ref, p2_ref, x_ref, g_ref, y_ref, coef_t, c_scr):
    j = pl.program_id(1)
    last = pl.num_programs(1) - 1
    ec = u_ref.shape[0]

    def build_c(el):
        for rt in range(PEER_KEYS // C_ROWS):
            rows = slice(C_ROWS * rt, C_ROWS * (rt + 1))
            c = None
            for h in range(PEER_HEADS):
                term = jnp.where(s2_ref[h, rows, :] > thr_ref[h, el:el + 1, :],
                                 p2_ref[h, rows, :] * cf_ref[h, el:el + 1, :], 0.0)
                c = term if c is None else c + term
            c_scr[PEER_KEYS * el + C_ROWS * rt:PEER_KEYS * el + C_ROWS * (rt + 1), :] = c

    @pl.when(j == 0)
    def _():
        y_ref[...] = x_ref[...]
        for el in range(ec // PEER_KEYS):
            build_c(el)

    @pl.when(j > 0)
    def _():
        for sb in range(ec // PEER_SUB):
            rows = slice(PEER_SUB * sb, PEER_SUB * (sb + 1))
            act = _dot(u_ref[rows, :], ht_ref[...])
            coef_t[:, rows] = (c_scr[rows, :] * _gelu(act)).T.astype(BF16)
        for ns in range(D_MODEL // V_SLICE):
            cols = slice(V_SLICE * ns, V_SLICE * (ns + 1))
            y_ref[:, cols] += _dot(coef_t[...], v_ref[:, cols])
            build_c(ns)

    @pl.when(j == last)
    def _():
        y_ref[...] = _rms(y_ref[...], g_ref[...])


def peer_mix(ht, u, v, thr, cf, s2, p2, x, g, tm):
    t = x.shape[0]
    ec = PEER_CHUNK
    nc = PEER_EXPERTS // ec
    assert D_MODEL // V_SLICE == ec // PEER_KEYS
    rows = pl.BlockSpec((PEER_HEADS, ec // PEER_KEYS, tm), lambda i, j: (0, jnp.minimum(j, nc - 1), i))
    full = pl.BlockSpec((PEER_HEADS, PEER_KEYS, tm), lambda i, j: (0, 0, i))
    chunk = pl.BlockSpec((ec, D_MODEL), lambda i, j: (jnp.maximum(j - 1, 0), 0))
    return pl.pallas_call(
        _peer_mix_kernel,
        grid=(t // tm, nc + 1),
        in_specs=[pl.BlockSpec((D_MODEL, tm), lambda i, j: (0, i)), chunk, chunk,
                  rows, rows, full, full,
                  pl.BlockSpec((tm, D_MODEL), lambda i, j: (i, 0)),
                  pl.BlockSpec((1, D_MODEL), lambda i, j: (0, 0))],
        out_specs=pl.BlockSpec((tm, D_MODEL), lambda i, j: (i, 0)),
        out_shape=jax.ShapeDtypeStruct((t, D_MODEL), F32),
        scratch_shapes=[pltpu.VMEM((tm, ec), BF16), pltpu.VMEM((ec, tm), F32)],
        compiler_params=_params(2),
        name="peer_mix",
    )(ht, u, v, thr, cf, s2, p2, x, g.reshape(1, D_MODEL))


def _block_diag(w, per_tile):
    nb, bw, _ = w.shape
    w4 = w.reshape(nb // per_tile, per_tile, bw, bw)
    eye = jnp.eye(per_tile, dtype=w.dtype)
    tiles = w4[:, :, :, None, :] * eye[None, :, None, :, None]
    return tiles.reshape(nb // per_tile, per_tile * bw, per_tile * bw).astype(BF16)


def _rope_tables(pos):
    half = ROPE_DIM // 2
    inv = ROPE_THETA ** (-jnp.arange(half, dtype=F32) / half)
    ang = pos.astype(F32)[:, None] * inv[None, :]
    cos, sin = jnp.cos(ang), jnp.sin(ang)
    pad = jnp.zeros((pos.shape[0], LANES - ROPE_DIM), F32)
    return (jnp.concatenate([cos, cos, pad], axis=1), jnp.concatenate([-sin, sin, pad], axis=1))


def _prepare_weights(w_in, conv_w, conv_b, w_rec_a, b_rec_a, w_rec_x, b_rec_x, lru_lambda, w_o_lru,
                     g_q_a, w_q_b, g_kv_a, w_kv_b, w_o_mla, w_mem_k, w_mem_v, w_o_mem, w_out,
                     w_peer_query, peer_sub_keys, peer_u, peer_v):
    offs = [0] + np.cumsum([D_RNN, D_RNN, Q_LORA, KV_LORA + ROPE_DIM, MEM_HEADS * MEM_HEAD_DIM]).tolist() + [w_in.shape[1]]
    w_xr, w_gr, w_qa, w_kva, w_qm, w_gates = (w_in[:, a:b] for a, b in zip(offs[:-1], offs[1:]))
    pad = jnp.zeros((D_MODEL, P_WIDTH - OFF_KR - ROPE_DIM), w_in.dtype)
    w_p = jnp.concatenate([w_gates, w_xr, w_gr, w_qm, w_qa, w_kva, pad], axis=1).astype(BF16)
    row = lambda v: v.reshape(1, -1)
    lw = dict(cw=conv_w, cb=row(conv_b), wa=_block_diag(w_rec_a, 4), ba=row(b_rec_a),
              wx=_block_diag(w_rec_x, 4), bx=row(b_rec_x), lam=row(lru_lambda))
    w_pe = jnp.pad(w_q_b[:, :, NOPE_DIM:], ((0, 0), (0, 0), (0, LANES - ROPE_DIM)))
    mw = dict(gq=row(g_q_a), gkv=row(g_kv_a),
              wn=w_q_b[:, :, :NOPE_DIM].reshape(Q_LORA, MLA_HEADS * NOPE_DIM).astype(BF16),
              wp=w_pe.reshape(Q_LORA, MLA_HEADS * LANES).astype(BF16),
              wkn=jnp.transpose(w_kv_b[:, :, :NOPE_DIM], (1, 2, 0)).astype(BF16),
              wv=jnp.transpose(w_kv_b[:, :, NOPE_DIM:], (1, 0, 2)).astype(BF16))
    ow = dict(wl=w_o_lru.astype(BF16), wa=w_o_mla.astype(BF16), wm=w_o_mem.astype(BF16), wo=w_out.astype(BF16))
    w_mem = jnp.concatenate([w_mem_k.reshape(D_MODEL, -1), w_mem_v.reshape(D_MODEL, -1)], axis=1).astype(BF16)
    pw = dict(wq=w_peer_query.reshape(D_MODEL, -1).astype(BF16), keys=peer_sub_keys.astype(BF16),
              u=peer_u.astype(BF16), v=peer_v.astype(BF16))
    return w_p, lw, mw, ow, w_mem, pw


def _peer_and_norm(x2, g_ffn, g_final, pw, tm):
    ht, thr, cf, s2, p2 = peer_route(x2, g_ffn, pw["wq"], pw["keys"], tm)
    return peer_mix(ht, pw["u"], pw["v"], thr, cf, s2, p2, x2, g_final, tm)


def kernel(x_prompt, x_sample, cache_mla_ckv, cache_mla_krope, cache_mem_k, cache_mem_v, state_lru_h, state_lru_conv, page_table, mem_prompt, g_mix, w_in, conv_w, conv_b, w_rec_a, b_rec_a, w_rec_x, b_rec_x, lru_lambda, w_o_lru, g_q_a, w_q_b, g_kv_a, w_kv_b, w_o_mla, g_mem, w_mem_k, w_mem_v, w_o_mem, w_out, g_ffn, w_peer_query, peer_sub_keys, peer_u, peer_v, g_final):
    batch, seq, _ = x_prompt.shape
    dec_batch, dec_seq, _ = x_sample.shape
    past_len = page_table.shape[1] * PAGE_SIZE
    tp, ts = batch * seq, dec_batch * dec_seq
    tm = 512
    mem_width = MEM_HEADS * MEM_HEAD_DIM

    w_p, lw, mw, ow, w_mem, pw = _prepare_weights(
        w_in, conv_w, conv_b, w_rec_a, b_rec_a, w_rec_x, b_rec_x, lru_lambda, w_o_lru, g_q_a, w_q_b, g_kv_a,
        w_kv_b, w_o_mla, w_mem_k, w_mem_v, w_o_mem, w_out, w_peer_query, peer_sub_keys, peer_u, peer_v)

    xp = x_prompt.reshape(tp, D_MODEL)
    head_shape = (MEM_TOKENS, MEM_HEADS, MEM_HEAD_DIM)
    pp = norm_matmul(xp, g_mix, w_p, tm, PROJ_TN, "in_proj")
    mem_kv = norm_matmul(mem_prompt.reshape(batch * MEM_TOKENS, D_MODEL), g_mem, w_mem, tm, PROJ_TN, "mem_kv")
    mem_k_p = mem_kv[:, :mem_width].reshape(batch, MEM_TOKENS, mem_width)
    mem_v_p = mem_kv[:, mem_width:].reshape(batch, MEM_TOKENS, mem_width)
    y_lru_p, lru_h_p = lru_prompt(pp, batch, seq, lw)
    cos_p, sin_p = _rope_tables(jnp.tile(jnp.arange(seq), batch))
    q_p, kcat_p, ckv_p, kr_p = mla_prep(pp, cos_p, sin_p, mw, tm)
    o_mla_p = attn_prompt(q_p, kcat_p, mw["wv"], batch, seq)
    o_mem_p = mem_attn(pp, OFF_QM // mem_width, mem_k_p, mem_v_p, seq, tm)
    x2_p = merge(xp, pp, y_lru_p, o_mla_p, o_mem_p, ow, tm)
    y_p = _peer_and_norm(x2_p, g_ffn, g_final, pw, tm)

    xs = x_sample.reshape(ts, D_MODEL)
    ps = norm_matmul(xs, g_mix, w_p, tm, PROJ_TN, "in_proj")
    step_major = lambda a: jnp.transpose(a.reshape(dec_batch, dec_seq, -1), (1, 0, 2))
    xr_s = ps[:, OFF_XR:OFF_XR + D_RNN]
    y_lru_s, lru_h_s = lru_sample(step_major(xr_s), step_major(ps[:, OFF_GR:OFF_GR + D_RNN]),
                                  jnp.transpose(state_lru_conv, (1, 0, 2)), state_lru_h, lw)
    y_lru_s = jnp.transpose(y_lru_s, (1, 0, 2)).reshape(ts, D_RNN)
    cos_s, sin_s = _rope_tables(jnp.tile(past_len + jnp.arange(dec_seq), dec_batch))
    q_s, kcat_s, ckv_s, kr_s = mla_prep(ps, cos_s, sin_s, mw, tm)
    q_rows = jnp.transpose(q_s.reshape(MLA_HEADS, dec_batch, dec_seq, QK_PAD), (1, 0, 2, 3))
    q_rows = q_rows.reshape(dec_batch, MLA_HEADS * dec_seq, QK_PAD)
    k_new = jnp.pad(kcat_s.reshape(dec_batch, dec_seq, QK_PAD), ((0, 0), (0, PAGE_SIZE - dec_seq), (0, 0)))
    o_lat = attn_sample(page_table, q_rows, k_new, cache_mla_ckv, jnp.swapaxes(cache_mla_krope, 1, 2))
    o_lat = jnp.transpose(o_lat.reshape(dec_batch, MLA_HEADS, dec_seq, KV_LORA), (1, 0, 2, 3))
    o_mla_s = vproj(o_lat.reshape(MLA_HEADS, ts, KV_LORA).astype(BF16), mw["wv"])
    q_pad_rows = 16
    qm_s = jnp.pad(ps[:, OFF_QM:OFF_QM + mem_width].reshape(dec_batch, dec_seq, mem_width),
                   ((0, 0), (0, q_pad_rows - dec_seq), (0, 0)))
    o_mem_s = mem_attn_cache(qm_s, cache_mem_k, cache_mem_v)[:, :dec_seq].reshape(ts, mem_width)
    x2_s = merge(xs, ps, y_lru_s, o_mla_s, o_mem_s, ow, tm)
    y_s = _peer_and_norm(x2_s, g_ffn, g_final, pw, tm)

    conv_tail = CONV_W - 1
    return (y_p.reshape(batch, seq, D_MODEL), y_s.reshape(dec_batch, dec_seq, D_MODEL),
            ckv_p.reshape(batch, seq, KV_LORA), kr_p[:, :ROPE_DIM].reshape(batch, seq, ROPE_DIM),
            mem_k_p.reshape((batch,) + head_shape), mem_v_p.reshape((batch,) + head_shape),
            lru_h_p, pp.reshape(batch, seq, P_WIDTH)[:, seq - conv_tail:, OFF_XR:OFF_XR + D_RNN],
            ckv_s.reshape(dec_batch, dec_seq, KV_LORA), kr_s[:, :ROPE_DIM].reshape(dec_batch, dec_seq, ROPE_DIM),
            lru_h_s, xr_s.reshape(dec_batch, dec_seq, D_RNN)[:, dec_seq - conv_tail:])
```
